```python
import jax, jax.numpy as jnp
from jax import lax
import numpy as np

D_MODEL = 2048
BATCH = 1
SEQ = 16384
DEPTH = 2

N_MIXERS = 2
MEM_TOKENS = 256
MEM_WIDTH = D_MODEL // 4
MEM_HEADS = 4
MEM_HEAD_DIM = MEM_WIDTH // MEM_HEADS
MIXER_WIDTH = D_MODEL - MEM_WIDTH
HEAD_DIM = 128
ATT_HEADS = MIXER_WIDTH // HEAD_DIM
ROPE_THETA = 500000.0
ROT_FRACTION = 4
IDX_HEADS = 16
IDX_DIM = 64
TOPK_MAX = 256
Q_BLOCK = 128
HG_HEADS = 12
HG_DK = MIXER_WIDTH // HG_HEADS
HG_DV = MIXER_WIDTH // HG_HEADS
HG_CHUNK = 64
D_FF = ((8 * D_MODEL // 3 + 255) // 256) * 256
CONV_WIDTH = 3
DSA_IN = 3 * MIXER_WIDTH + IDX_HEADS * IDX_DIM + IDX_DIM + IDX_HEADS + MEM_WIDTH
HG_IN = 4 * MIXER_WIDTH + MEM_WIDTH
N_DSA_LAYERS = (DEPTH + N_MIXERS - 1) // N_MIXERS
N_HG_LAYERS = DEPTH // N_MIXERS
NORM_EPS = 1e-6
NEG_INF = -1e30

kernel_name = 'hybrid_dsa_hgrn2_memory_convffn'


def rms_norm(x, w):
    xf = x.astype(jnp.float32)
    y = xf * lax.rsqrt(jnp.mean(xf * xf, axis=-1, keepdims=True) + NORM_EPS)
    return (y * w.astype(jnp.float32)).astype(x.dtype)


def split_cols(a, sizes):
    out, start = [], 0
    for s in sizes:
        out.append(a[..., start:start + s])
        start += s
    return out


def partial_rope(x, positions):
    d = x.shape[-1]
    rot = d // ROT_FRACTION
    half = rot // 2
    inv_freq = ROPE_THETA ** (-jnp.arange(half, dtype=jnp.float32) / half)
    ang = positions.astype(jnp.float32)[..., None] * inv_freq
    ang = ang.reshape(ang.shape[:2] + (1,) * (x.ndim - 3) + (half,))
    cos, sin = jnp.cos(ang), jnp.sin(ang)
    x1 = x[..., :half].astype(jnp.float32)
    x2 = x[..., half:rot].astype(jnp.float32)
    rotated = jnp.concatenate([x1 * cos - x2 * sin, x2 * cos + x1 * sin], axis=-1).astype(x.dtype)
    return jnp.concatenate([rotated, x[..., rot:]], axis=-1)


def dsa_sparse_attention(proj, positions):
    B, T, _ = proj.shape
    n_sel = min(TOPK_MAX, T // 4)
    nb = T // Q_BLOCK
    q, k, v, iq, ik, iw = split_cols(
        proj, [MIXER_WIDTH, MIXER_WIDTH, MIXER_WIDTH, IDX_HEADS * IDX_DIM, IDX_DIM, IDX_HEADS])
    q = partial_rope(q.reshape(B, T, ATT_HEADS, HEAD_DIM), positions)
    k = partial_rope(k.reshape(B, T, ATT_HEADS, HEAD_DIM), positions)
    v = v.reshape(B, T, ATT_HEADS, HEAD_DIM)
    iq = partial_rope(iq.reshape(B, T, IDX_HEADS, IDX_DIM), positions)
    ik = partial_rope(ik, positions)
    iw = iw * (IDX_HEADS ** -0.5 * IDX_DIM ** -0.5)
    key_pos = jnp.arange(T, dtype=jnp.int32)
    att_scale = HEAD_DIM ** -0.5

    def to_blocks(a):
        return jnp.moveaxis(a.reshape((B, nb, Q_BLOCK) + a.shape[2:]), 1, 0)

    def block(args):
        q_b, iq_b, iw_b, start = args
        q_pos = start + jnp.arange(Q_BLOCK, dtype=jnp.int32)
        s = jnp.einsum('bqhd,bsd->bqhs', iq_b, ik, preferred_element_type=jnp.float32)
        score = jnp.einsum('bqhs,bqh->bqs', jax.nn.relu(s), iw_b.astype(jnp.float32))
        causal = key_pos[None, :] <= q_pos[:, None]
        score = jnp.where(causal[None], score, NEG_INF)
        _, sel = lax.top_k(score, n_sel)
        valid = sel <= q_pos[None, :, None]
        k_g = jax.vmap(lambda kb, ib: kb[ib])(k, sel)
        v_g = jax.vmap(lambda vb, ib: vb[ib])(v, sel)
        logits = jnp.einsum('bqhd,bqkhd->bqhk', q_b, k_g, preferred_element_type=jnp.float32) * att_scale
        logits = jnp.where(valid[:, :, None, :], logits, NEG_INF)
        p = jax.nn.softmax(logits, axis=-1).astype(v.dtype)
        return jnp.einsum('bqhk,bqkhd->bqhd', p, v_g)

    starts = jnp.arange(nb, dtype=jnp.int32) * Q_BLOCK
    out = lax.map(block, (to_blocks(q), to_blocks(iq), to_blocks(iw), starts))
    return jnp.moveaxis(out, 0, 1).reshape(B, T, MIXER_WIDTH)


def hgrn_lower_bound(lb_logits, layer):
    cs = jnp.cumsum(jax.nn.softmax(lb_logits.astype(jnp.float32), axis=0), axis=0)
    return cs[layer] - cs[0]


def hgrn2_recurrence(proj, lower_bound, gnorm_w):
    B, T, _ = proj.shape
    q, f, i, og = split_cols(proj, [MIXER_WIDTH] * 4)
    q = jax.nn.silu(q.astype(jnp.float32))
    forget = lower_bound + (1.0 - lower_bound) * jax.nn.sigmoid(f.astype(jnp.float32))
    kin = 1.0 - forget
    g = jnp.log(forget)
    n = T // HG_CHUNK

    def to_chunks(a, d):
        return a.reshape(B, n, HG_CHUNK, HG_HEADS, d).transpose(1, 0, 3, 2, 4)

    qs, ks, gs = to_chunks(q, HG_DK), to_chunks(kin, HG_DK), to_chunks(g, HG_DK)
    vs = to_chunks(i.astype(jnp.float32), HG_DV)
    tri = jnp.tril(jnp.ones((HG_CHUNK, HG_CHUNK), dtype=bool))

    def step(S, xs):
        qc, kc, vc, gc = xs
        b = jnp.cumsum(gc, axis=2)
        o_inter = jnp.einsum('bhtk,bhkv->bhtv', qc * jnp.exp(b), S)
        diff = b[:, :, :, None, :] - b[:, :, None, :, :]
        decay = jnp.exp(jnp.where(tri[:, :, None], diff, -jnp.inf))
        a = jnp.einsum('bhtk,bhsk,bhtsk->bhts', qc, kc, decay)
        o_intra = jnp.einsum('bhts,bhsv->bhtv', a, vc)
        b_last = b[:, :, -1:, :]
        S = jnp.exp(b_last[:, :, 0, :, None]) * S + jnp.einsum(
            'bhsk,bhsv->bhkv', kc * jnp.exp(b_last - b), vc)
        return S, o_inter + o_intra

    S0 = jnp.zeros((B, HG_HEADS, HG_DK, HG_DV), jnp.float32)
    _, o = lax.scan(step, S0, (qs, ks, vs, gs))
    o = o.transpose(1, 0, 3, 2, 4).reshape(B, T, HG_HEADS, HG_DV)
    o = o * lax.rsqrt(jnp.mean(o * o, axis=-1, keepdims=True) + NORM_EPS) * gnorm_w.astype(jnp.float32)
    gate = jax.nn.silu(og.astype(jnp.float32)).reshape(B, T, HG_HEADS, HG_DV)
    return (o * gate).reshape(B, T, MIXER_WIDTH).astype(proj.dtype)


def memory_cross_attention(mq, mem_k, mem_v):
    B, T, _ = mq.shape
    q = mq.reshape(B, T, MEM_HEADS, MEM_HEAD_DIM)
    logits = jnp.einsum('bthd,bmhd->bhtm', q, mem_k, preferred_element_type=jnp.float32) * MEM_HEAD_DIM ** -0.5
    p = jax.nn.softmax(logits, axis=-1).astype(mem_v.dtype)
    return jnp.einsum('bhtm,bmhd->bthd', p, mem_v).reshape(B, T, MEM_WIDTH)


def causal_depthwise_conv(h, w, b):
    F = h.shape[-1]
    y = lax.conv_general_dilated(
        h, w[:, None, :].astype(h.dtype), window_strides=(1,), padding=[(CONV_WIDTH - 1, 0)],
        dimension_numbers=('NWC', 'WIO', 'NWC'), feature_group_count=F)
    return y + b.astype(h.dtype)


def conv_ffn(h, w_gate, w_up, conv_w, conv_b, w_down):
    gt = causal_depthwise_conv(h @ w_gate, conv_w, conv_b)
    return (jax.nn.silu(gt) * (h @ w_up)) @ w_down


def setup_inputs(seed: int = 0) -> dict:
    key = jax.random.key(seed)
    ks = jax.random.split(key, 20)
    f32 = jnp.float32

    def nrm(k, shape, fan_in):
        return jax.random.normal(k, shape, f32) * fan_in ** -0.5

    def gain(k, shape):
        return 1.0 + 0.02 * jax.random.normal(k, shape, f32)

    offset = jax.random.randint(ks[2], (BATCH, 1), 0, 4096, dtype=jnp.int32)
    positions = offset + jnp.arange(SEQ, dtype=jnp.int32)[None, :]
    return {
        'x': jax.random.normal(ks[0], (BATCH, SEQ, D_MODEL), f32),
        'mem': jax.random.normal(ks[1], (BATCH, MEM_TOKENS, D_MODEL), f32),
        'positions': positions,
        'ln1_w': gain(ks[3], (DEPTH, D_MODEL)),
        'ln2_w': gain(ks[4], (DEPTH, D_MODEL)),
        'dsa_w_in': nrm(ks[5], (N_DSA_LAYERS, D_MODEL, DSA_IN), D_MODEL),
        'hgrn_w_in': nrm(ks[6], (N_HG_LAYERS, D_MODEL, HG_IN), D_MODEL),
        'hgrn_lb_logits': 0.1 * jax.random.normal(ks[7], (DEPTH, HG_HEADS * HG_DK), f32),
        'hgrn_gnorm_w': gain(ks[8], (N_HG_LAYERS, HG_DV)),
        'mem_norm_w': gain(ks[9], (D_MODEL,)),
        'w_mem_kv': nrm(ks[10], (D_MODEL, 2 * MEM_WIDTH), D_MODEL),
        'w_out': nrm(ks[11], (DEPTH, D_MODEL, D_MODEL), D_MODEL),
        'ffn_w_gate': nrm(ks[12], (DEPTH, D_MODEL, D_FF), D_MODEL),
        'ffn_w_up': nrm(ks[13], (DEPTH, D_MODEL, D_FF), D_MODEL),
        'ffn_conv_w': nrm(ks[14], (DEPTH, CONV_WIDTH, D_FF), CONV_WIDTH),
        'ffn_conv_b': 0.01 * jax.random.normal(ks[15], (DEPTH, D_FF), f32),
        'ffn_w_down': nrm(ks[16], (DEPTH, D_FF, D_MODEL), D_FF),
        'final_norm_w': gain(ks[17], (D_MODEL,)),
    }


def reference(x, mem, positions, ln1_w, ln2_w, dsa_w_in, hgrn_w_in, hgrn_lb_logits, hgrn_gnorm_w,
              mem_norm_w, w_mem_kv, w_out, ffn_w_gate, ffn_w_up, ffn_conv_w, ffn_conv_b, ffn_w_down,
              final_norm_w):
    B, M, _ = mem.shape
    mem_kv = rms_norm(mem, mem_norm_w) @ w_mem_kv
    mem_k = mem_kv[..., :MEM_WIDTH].reshape(B, M, MEM_HEADS, MEM_HEAD_DIM)
    mem_v = mem_kv[..., MEM_WIDTH:].reshape(B, M, MEM_HEADS, MEM_HEAD_DIM)
    for layer in range(DEPTH):
        h = rms_norm(x, ln1_w[layer])
        j = layer // N_MIXERS
        if layer % N_MIXERS == 0:
            proj = h @ dsa_w_in[j]
            mix = dsa_sparse_attention(proj, positions)
        else:
            proj = h @ hgrn_w_in[j]
            mix = hgrn2_recurrence(proj, hgrn_lower_bound(hgrn_lb_logits, layer), hgrn_gnorm_w[j])
        mem_out = memory_cross_attention(proj[..., -MEM_WIDTH:], mem_k, mem_v)
        x = x + jnp.concatenate([mix, mem_out], axis=-1) @ w_out[layer]
        h = rms_norm(x, ln2_w[layer])
        x = x + conv_ffn(h, ffn_w_gate[layer], ffn_w_up[layer], ffn_conv_w[layer], ffn_conv_b[layer],
                         ffn_w_down[layer])
    return rms_norm(x, final_norm_w)
```

```python
import functools

import numpy as np
import jax
import jax.numpy as jnp
from jax import lax
from jax.experimental import pallas as pl
from jax.experimental.pallas import tpu as pltpu

F32 = jnp.float32
BF16 = jnp.bfloat16
I32 = jnp.int32

LANES = 128
HEAD_DIM = 128
IDX_HEADS = 16
IDX_DIM = 64
TOPK_MAX = 256
ROPE_THETA = 500000.0
ROT_FRACTION = 4
N_MIXERS = 2
CONV_WIDTH = 3
NORM_EPS = 1e-6
NEG_INF = -1e30
INT_MIN = -(2 ** 31)

V7X_VMEM_LIMIT_BYTES = 56 * 1024 * 1024

NT_DIMS = (((1,), (1,)), ((), ()))


def _cparams(*sem):
    return pltpu.CompilerParams(dimension_semantics=sem,
                                vmem_limit_bytes=V7X_VMEM_LIMIT_BYTES)


def _pick_tile(n, cap, mult=LANES):
    best = None
    t = mult
    while t <= min(n, cap):
        if n % t == 0:
            best = t
        t += mult
    assert best is not None, (n, cap, mult)
    return best


def _sigmoid(x):
    return 1.0 / (1.0 + jnp.exp(-x))


def _rms_rows(x_ref, g_ref, out_ref, chunk=64):
    rows = x_ref.shape[0]
    chunk = min(chunk, rows)

    def body(c, carry):
        r = pl.multiple_of(c * chunk, chunk)
        x = x_ref[pl.ds(r, chunk), :].astype(F32)
        ms = jnp.mean(x * x, axis=-1, keepdims=True)
        out_ref[pl.ds(r, chunk), :] = (x * lax.rsqrt(ms + NORM_EPS) * g_ref[...]).astype(out_ref.dtype)
        return carry

    lax.fori_loop(0, rows // chunk, body, 0)


def _rope_lane_constants():
    def pattern(period, rot):
        half = rot // 2
        inv = (np.float32(ROPE_THETA) ** (-(np.arange(half, dtype=np.float32) / np.float32(half)))).astype(np.float32)
        lane = np.arange(LANES) % period
        freq = np.where(lane < rot, inv[lane % half], 0.0).astype(np.float32)
        sign = np.where(lane < half, -1.0, np.where(lane < rot, 1.0, 0.0)).astype(np.float32)
        return freq, sign
    f_att, s_att = pattern(HEAD_DIM, HEAD_DIM // ROT_FRACTION)
    f_idx, s_idx = pattern(IDX_DIM, IDX_DIM // ROT_FRACTION)
    return (np.concatenate([f_att, f_idx])[None, :], np.concatenate([s_att, s_idx])[None, :])


def _rope_tables_kernel(pos_ref, f_ref, s_ref, cos_ref, sin_ref):
    ang = pos_ref[...].astype(F32) * f_ref[...]
    cos_ref[...] = jnp.cos(ang)
    sin_ref[...] = jnp.sin(ang) * s_ref[...]


def _rope_tables(pos_col):
    T = pos_col.shape[0]
    tm = _pick_tile(T, 512, 8)
    freq, sign = _rope_lane_constants()
    row = pl.BlockSpec((tm, 2 * LANES), lambda i: (i, 0))
    const = pl.BlockSpec((1, 2 * LANES), lambda i: (0, 0))
    return pl.pallas_call(
        _rope_tables_kernel,
        grid=(T // tm,),
        in_specs=[pl.BlockSpec((tm, 1), lambda i: (i, 0)), const, const],
        out_specs=[row, row],
        out_shape=[jax.ShapeDtypeStruct((T, 2 * LANES), F32)] * 2,
        compiler_params=_cparams("parallel"),
        name="rope_tables",
    )(pos_col, jnp.asarray(freq), jnp.asarray(sign))


def _norm_matmul_kernel(x_ref, g_ref, w_ref, o_ref, h_ref):
    @pl.when(pl.program_id(1) == 0)
    def _():
        _rms_rows(x_ref, g_ref, h_ref)

    o_ref[...] = jnp.dot(h_ref[...], w_ref[...], preferred_element_type=F32).astype(o_ref.dtype)


def _norm_matmul(x, g, w, *, out_dtype, tm_cap=512, tn_cap=1024, name):
    T, D = x.shape
    N = w.shape[1]
    tm = _pick_tile(T, tm_cap, 8)
    tn = _pick_tile(N, tn_cap)
    return pl.pallas_call(
        _norm_matmul_kernel,
        grid=(T // tm, N // tn),
        in_specs=[pl.BlockSpec((tm, D), lambda i, j: (i, 0)),
                  pl.BlockSpec((1, D), lambda i, j: (0, 0)),
                  pl.BlockSpec((D, tn), lambda i, j: (0, j))],
        out_specs=pl.BlockSpec((tm, tn), lambda i, j: (i, j)),
        out_shape=jax.ShapeDtypeStruct((T, N), out_dtype),
        scratch_shapes=[pltpu.VMEM((tm, D), BF16)],
        compiler_params=_cparams("parallel", "arbitrary"),
        name=name,
    )(x, g.reshape(1, D), w)


def _rope_apply(x, cos, sin, half, period):
    lane = lax.broadcasted_iota(I32, x.shape, 1)
    up = pltpu.roll(x, LANES - half, 1)
    dn = pltpu.roll(x, half, 1)
    partner = jnp.where((lane % period) < half, up, dn)
    return x * cos + partner * sin


def _dsa_prep_kernel(p_ref, cos_ref, sin_ref, q_ref, k_ref, v_ref, iq_ref, ika_ref, ikb_ref, iw_ref, *, H):
    Wm = H * HEAD_DIM
    G = IDX_HEADS * IDX_DIM // LANES
    cos_a, sin_a = cos_ref[:, :LANES], sin_ref[:, :LANES]
    cos_i, sin_i = cos_ref[:, LANES:], sin_ref[:, LANES:]
    half_a = HEAD_DIM // ROT_FRACTION // 2
    half_i = IDX_DIM // ROT_FRACTION // 2
    for h in range(H):
        q = p_ref[:, h * LANES:(h + 1) * LANES]
        q_ref[h] = _rope_apply(q, cos_a, sin_a, half_a, HEAD_DIM).astype(BF16)
        k = p_ref[:, Wm + h * LANES:Wm + (h + 1) * LANES]
        k_ref[h] = _rope_apply(k, cos_a, sin_a, half_a, HEAD_DIM).astype(BF16)
        v_ref[h] = p_ref[:, 2 * Wm + h * LANES:2 * Wm + (h + 1) * LANES].astype(BF16)
    for g in range(G):
        x = p_ref[:, 3 * Wm + g * LANES:3 * Wm + (g + 1) * LANES]
        iq_ref[g] = _rope_apply(x, cos_i, sin_i, half_i, IDX_DIM).astype(BF16)
    tail = p_ref[:, p_ref.shape[1] - LANES:]
    lane = lax.broadcasted_iota(I32, tail.shape, 1)
    ik = jnp.where(lane < IDX_DIM, _rope_apply(tail, cos_i, sin_i, half_i, IDX_DIM), 0.0)
    ika_ref[...] = ik.astype(BF16)
    ikb_ref[...] = pltpu.roll(ik, IDX_DIM, 1).astype(BF16)
    iw = pltpu.roll(tail, LANES - IDX_DIM, 1)
    iw_ref[...] = jnp.where(lane < IDX_HEADS, iw * (IDX_HEADS ** -0.5 * IDX_DIM ** -0.5), 0.0)


def _dsa_prep(proj, cos_t, sin_t, *, H):
    T, N = proj.shape
    G = IDX_HEADS * IDX_DIM // LANES
    tm = _pick_tile(T, 256, 16)
    hm = lambda n: pl.BlockSpec((n, tm, LANES), lambda i: (0, i, 0))
    row = pl.BlockSpec((tm, LANES), lambda i: (i, 0))
    tab = pl.BlockSpec((tm, 2 * LANES), lambda i: (i, 0))
    return pl.pallas_call(
        functools.partial(_dsa_prep_kernel, H=H),
        grid=(T // tm,),
        in_specs=[pl.BlockSpec((tm, N), lambda i: (i, 0)), tab, tab],
        out_specs=[hm(H), hm(H), hm(H), hm(G), row, row, row],
        out_shape=[jax.ShapeDtypeStruct((H, T, LANES), BF16)] * 3
        + [jax.ShapeDtypeStruct((G, T, LANES), BF16)]
        + [jax.ShapeDtypeStruct((T, LANES), BF16)] * 2
        + [jax.ShapeDtypeStruct((T, LANES), F32)],
        compiler_params=_cparams("parallel"),
        name="dsa_prep",
    )(proj, cos_t, sin_t)


def _indexer_kernel(iq_ref, ika_ref, ikb_ref, iw_ref, mask_ref, key_ref, iwb_ref, s_ref, *, bq, kc, n_sel):
    i = pl.program_id(0)
    G = iq_ref.shape[0]
    nkc = key_ref.shape[0]
    reps = kc // LANES
    n_c = ((i + 1) * bq + kc - 1) // kc

    iw = iw_ref[...]
    for h in range(IDX_HEADS):
        iwb_ref[h] = jnp.broadcast_to(iw[:, h:h + 1], (bq, LANES))

    iq = iq_ref[...].reshape(G * bq, LANES)
    qpos = i * bq + lax.broadcasted_iota(I32, (bq, kc), 0)
    lane_k = lax.broadcasted_iota(I32, (bq, kc), 1)

    def score_chunk(c, carry):
        k0 = pl.multiple_of(c * kc, kc)
        score = jnp.zeros((bq, kc), F32)
        for half, ik_ref in enumerate((ika_ref, ikb_ref)):
            s_ref[...] = lax.dot_general(iq, ik_ref[pl.ds(k0, kc), :], NT_DIMS, preferred_element_type=F32)
            for g in range(G):
                w = pltpu.repeat(iwb_ref[2 * g + half], reps, 1)
                score = score + jnp.maximum(s_ref[g * bq:(g + 1) * bq, :], 0.0) * w
        score = jnp.where(k0 + lane_k <= qpos, score, NEG_INF)
        bits = pltpu.bitcast(score, I32)
        key_ref[c] = bits ^ ((bits >> 31) & 0x7FFFFFFF)
        return carry

    lax.fori_loop(0, n_c, score_chunk, 0)

    def count_ge(cmp):
        cmp_b = jnp.broadcast_to(cmp, (bq, LANES))

        def body(c, acc):
            k = key_ref[c]
            for l in range(reps):
                acc = acc + jnp.where(k[:, l * LANES:(l + 1) * LANES] >= cmp_b, 1.0, 0.0)
            return acc

        acc = lax.fori_loop(0, n_c, body, jnp.zeros((bq, LANES), F32))
        return jnp.sum(acc, axis=1, keepdims=True)

    def bit_step(b, t):
        cand = t | jnp.left_shift(jnp.int32(1), 31 - b)
        cnt = count_ge(cand ^ INT_MIN)
        return jnp.where(cnt >= n_sel, cand, t)

    t = lax.fori_loop(0, 32, bit_step, jnp.zeros((bq, 1), I32))
    thr = jnp.broadcast_to(t ^ INT_MIN, (bq, kc))

    def emit(c, carry):
        k0 = pl.multiple_of(c * kc, kc)
        sel = jnp.where(k0 + lane_k <= qpos, jnp.where(key_ref[c] >= thr, 1, 0), 0)
        mask_ref[0, c] = sel.astype(jnp.int8)
        return carry

    lax.fori_loop(0, n_c, emit, 0)

    def clear(c, carry):
        mask_ref[0, c] = jnp.zeros((bq, kc), jnp.int8)
        return carry

    lax.fori_loop(n_c, nkc, clear, 0)


def _indexer(iq, ika, ikb, iw, *, bq, kc, n_sel):
    G, T, _ = iq.shape
    nq, nkc = T // bq, T // kc
    full = pl.BlockSpec((T, LANES), lambda i: (0, 0))
    row = pl.BlockSpec((bq, LANES), lambda i: (i, 0))
    return pl.pallas_call(
        functools.partial(_indexer_kernel, bq=bq, kc=kc, n_sel=n_sel),
        grid=(nq,),
        in_specs=[pl.BlockSpec((G, bq, LANES), lambda i: (0, i, 0)), full, full, row],
        out_specs=pl.BlockSpec((1, nkc, bq, kc), lambda i: (i, 0, 0, 0)),
        out_shape=jax.ShapeDtypeStruct((nq, nkc, bq, kc), jnp.int8),
        scratch_shapes=[pltpu.VMEM((nkc, bq, kc), I32),
                        pltpu.VMEM((IDX_HEADS, bq, LANES), F32),
                        pltpu.VMEM((G * bq, kc), F32)],
        compiler_params=_cparams("parallel"),
        name="dsa_indexer",
    )(iq, ika, ikb, iw)


def _flash_kernel(qi_tab, ki_tab, q_ref, k_ref, v_ref, mask_ref, o_ref, m_ref, l_ref, acc_ref, bias_ref,
                  *, bq, bk, mq_rows):
    s = pl.program_id(0)
    qi, ki = qi_tab[s], ki_tab[s]
    H = q_ref.shape[0]
    reps = bk // LANES
    scale = HEAD_DIM ** -0.5

    @pl.when(ki == 0)
    def _():
        m_ref[...] = jnp.full(m_ref.shape, NEG_INF, F32)
        l_ref[...] = jnp.zeros(l_ref.shape, F32)
        acc_ref[...] = jnp.zeros(acc_ref.shape, F32)

    for r in range(bq // mq_rows):
        sel = mask_ref[r, 0].astype(F32)
        bias_ref[r * mq_rows:(r + 1) * mq_rows, :] = (sel - 1.0) * (-NEG_INF)

    def head(h, carry):
        logits = lax.dot_general(q_ref[h], k_ref[h], NT_DIMS, preferred_element_type=F32) * scale + bias_ref[...]
        m_prev, l_prev = m_ref[h], l_ref[h]
        m_next = jnp.maximum(m_prev, jnp.max(logits, axis=1, keepdims=True))
        p = jnp.exp(logits - pltpu.repeat(m_next, reps, 1))
        alpha = jnp.exp(m_prev - m_next)
        l_ref[h] = alpha * l_prev + jnp.sum(p, axis=1, keepdims=True)
        m_ref[h] = m_next
        acc_ref[h] = acc_ref[h] * alpha + jnp.dot(p.astype(BF16), v_ref[h], preferred_element_type=F32)
        return carry

    lax.fori_loop(0, H, head, 0)

    @pl.when(ki == ((qi + 1) * bq - 1) // bk)
    def _():
        for h in range(H):
            o_ref[:, h * LANES:(h + 1) * LANES] = (acc_ref[h] / l_ref[h]).astype(o_ref.dtype)


def _flash(qh, kh, vh, mask, *, bq, bk):
    H, T, _ = qh.shape
    mq_rows = mask.shape[2]
    assert mask.shape[3] == bk and bq % mq_rows == 0
    steps = [(qi, ki) for qi in range(T // bq) for ki in range(((qi + 1) * bq - 1) // bk + 1)]
    qi_tab = jnp.asarray(np.array([s[0] for s in steps], np.int32))
    ki_tab = jnp.asarray(np.array([s[1] for s in steps], np.int32))
    grid_spec = pltpu.PrefetchScalarGridSpec(
        num_scalar_prefetch=2,
        grid=(len(steps),),
        in_specs=[pl.BlockSpec((H, bq, LANES), lambda s, qt, kt: (0, qt[s], 0)),
                  pl.BlockSpec((H, bk, LANES), lambda s, qt, kt: (0, kt[s], 0)),
                  pl.BlockSpec((H, bk, LANES), lambda s, qt, kt: (0, kt[s], 0)),
                  pl.BlockSpec((bq // mq_rows, 1, mq_rows, bk), lambda s, qt, kt: (qt[s], kt[s], 0, 0))],
        out_specs=pl.BlockSpec((bq, H * LANES), lambda s, qt, kt: (qt[s], 0)),
        scratch_shapes=[pltpu.VMEM((H, bq, LANES), F32),
                        pltpu.VMEM((H, bq, LANES), F32),
                        pltpu.VMEM((H, bq, LANES), F32),
                        pltpu.VMEM((bq, bk), F32)],
    )
    return pl.pallas_call(
        functools.partial(_flash_kernel, bq=bq, bk=bk, mq_rows=mq_rows),
        grid_spec=grid_spec,
        out_shape=jax.ShapeDtypeStruct((T, H * LANES), BF16),
        compiler_params=_cparams("arbitrary"),
        name="dsa_flash",
    )(qi_tab, ki_tab, qh, kh, vh, mask)


def _hgrn_static(C):
    levels = []
    m = C // 2
    while m >= 1:
        levels.append(m)
        m //= 2
    r = np.arange(C)[:, None]
    u = np.arange(C)[None, :]
    mats = [u <= r, u > r]
    masks = []
    for m in levels:
        mid = (r // (2 * m)) * 2 * m + m - 1
        lower = (r % (2 * m)) >= m
        mats.append(np.where(lower, (u > mid) & (u <= r), (u > r) & (u <= mid)))
        masks.append(((r // (2 * m)) == (u // (2 * m))) & lower & ((u % (2 * m)) < m))
    W = np.concatenate(mats, 0).astype(np.float32)
    return np.concatenate([W, W, W], 1), np.stack(masks).astype(np.float32)


def _hgrn_kernel(q_ref, f_ref, i_ref, og_ref, lb_ref, gw_ref, w3_ref, msk_ref, o_ref, st_ref, e_ref,
                 *, C, hp, layer):
    L = msk_ref.shape[0]

    @pl.when(pl.program_id(1) == 0)
    def _():
        st_ref[...] = jnp.zeros(st_ref.shape, F32)

    depth = lb_ref.shape[0]
    rows = [lb_ref[d:d + 1, :] for d in range(depth)]
    mx = functools.reduce(jnp.maximum, rows)
    ex = [jnp.exp(rw - mx) for rw in rows]
    den = functools.reduce(lambda a, b: a + b, ex)
    lb = jnp.zeros_like(mx)
    for d in range(1, layer + 1):
        lb = lb + ex[d] / den

    forget = lb + (1.0 - lb) * _sigmoid(f_ref[...])
    kin = 1.0 - forget
    g = jnp.log(forget)
    g_hi = g.astype(BF16)
    r1 = g - g_hi.astype(F32)
    g_mid = r1.astype(BF16)
    g_lo = (r1 - g_mid.astype(F32)).astype(BF16)
    g3 = jnp.concatenate([g_hi, g_mid, g_lo], axis=0)
    e_ref[...] = jnp.dot(w3_ref[...], g3, preferred_element_type=F32)

    q = q_ref[...]
    qs = q * _sigmoid(q)
    og = og_ref[...]
    gate = og * _sigmoid(og)
    eye = lax.broadcasted_iota(I32, (C, C), 0) == lax.broadcasted_iota(I32, (C, C), 1)

    for hh in range(hp):
        sl = slice(hh * LANES, (hh + 1) * LANES)
        qh, kh, vh = qs[:, sl], kin[:, sl], i_ref[:, sl]
        st = st_ref[hh]
        o = lax.dot_general((qh * jnp.exp(e_ref[0:C, sl])).astype(BF16), st.astype(BF16), NT_DIMS,
                            preferred_element_type=F32)
        a = jnp.where(eye, jnp.sum(qh * kh, axis=1, keepdims=True), 0.0)
        for l in range(L):
            el = jnp.exp(e_ref[(2 + l) * C:(3 + l) * C, sl])
            a = a + msk_ref[l] * lax.dot_general((qh * el).astype(BF16), (kh * el).astype(BF16), NT_DIMS,
                                                 preferred_element_type=F32)
        vb = vh.astype(BF16)
        o = o + jnp.dot(a.astype(BF16), vb, preferred_element_type=F32)
        kd = (kh * jnp.exp(e_ref[C:2 * C, sl])).astype(BF16)
        st_ref[hh] = st * jnp.exp(e_ref[C - 1:C, sl]) + jnp.dot(vh.T.astype(BF16), kd, preferred_element_type=F32)
        o = o * lax.rsqrt(jnp.mean(o * o, axis=1, keepdims=True) + NORM_EPS) * gw_ref[...]
        o_ref[:, sl] = (o * gate[:, sl]).astype(o_ref.dtype)


def _hgrn(proj, lb_logits, gnorm_w, *, H, layer, C=128, hp=2):
    T = proj.shape[0]
    hp = hp if H % hp == 0 else 1
    wb = hp * LANES
    nb = H // hp
    w3, masks = _hgrn_static(C)
    L = masks.shape[0]
    col = lambda k: pl.BlockSpec((C, wb), lambda hb, c: (c, hb + k * nb))
    return pl.pallas_call(
        functools.partial(_hgrn_kernel, C=C, hp=hp, layer=layer),
        grid=(nb, T // C),
        in_specs=[col(0), col(1), col(2), col(3),
                  pl.BlockSpec((lb_logits.shape[0], wb), lambda hb, c: (0, hb)),
                  pl.BlockSpec((1, LANES), lambda hb, c: (0, 0)),
                  pl.BlockSpec(w3.shape, lambda hb, c: (0, 0)),
                  pl.BlockSpec(masks.shape, lambda hb, c: (0, 0, 0))],
        out_specs=pl.BlockSpec((C, wb), lambda hb, c: (c, hb)),
        out_shape=jax.ShapeDtypeStruct((T, H * LANES), BF16),
        scratch_shapes=[pltpu.VMEM((hp, LANES, LANES), F32),
                        pltpu.VMEM(((2 + L) * C, wb), F32)],
        compiler_params=_cparams("parallel", "arbitrary"),
        name="hgrn2",
    )(proj, proj, proj, proj, lb_logits, gnorm_w.reshape(1, LANES), jnp.asarray(w3, BF16), jnp.asarray(masks))


def _outproj_kernel(mix_ref, mq_ref, mk_ref, mv_ref, w_ref, x_ref, o_ref, cat_ref):
    Wm = mix_ref.shape[1]
    scale = HEAD_DIM ** -0.5
    cat_ref[:, :Wm] = mix_ref[...]
    for h in range(mq_ref.shape[1] // LANES):
        sl = slice(h * LANES, (h + 1) * LANES)
        logits = lax.dot_general(mq_ref[:, sl].astype(BF16), mk_ref[:, sl], NT_DIMS,
                                 preferred_element_type=F32) * scale
        p = jnp.exp(logits - jnp.max(logits, axis=1, keepdims=True))
        den = jnp.sum(p, axis=1, keepdims=True)
        o = jnp.dot(p.astype(BF16), mv_ref[:, sl], preferred_element_type=F32) / den
        cat_ref[:, Wm + h * LANES:Wm + (h + 1) * LANES] = o.astype(BF16)
    o_ref[...] = x_ref[...] + jnp.dot(cat_ref[...], w_ref[...], preferred_element_type=F32)


def _outproj(mix, proj, mq_block, mem_kv, w_out, x):
    T, D = x.shape
    Wm = mix.shape[1]
    Wmem = D - Wm
    M = mem_kv.shape[0]
    tm = _pick_tile(T, 512, 16)
    return pl.pallas_call(
        _outproj_kernel,
        grid=(T // tm,),
        in_specs=[pl.BlockSpec((tm, Wm), lambda i: (i, 0)),
                  pl.BlockSpec((tm, Wmem), lambda i: (i, mq_block)),
                  pl.BlockSpec((M, Wmem), lambda i: (0, 0)),
                  pl.BlockSpec((M, Wmem), lambda i: (0, 1)),
                  pl.BlockSpec((D, D), lambda i: (0, 0)),
                  pl.BlockSpec((tm, D), lambda i: (i, 0))],
        out_specs=pl.BlockSpec((tm, D), lambda i: (i, 0)),
        out_shape=jax.ShapeDtypeStruct((T, D), F32),
        scratch_shapes=[pltpu.VMEM((tm, D), BF16)],
        compiler_params=_cparams("parallel"),
        name="outproj",
    )(mix, proj, mem_kv, mem_kv, w_out, x)


def _ffn_kernel(x_ref, xh_ref, g_ref, wg_ref, wu_ref, cw_ref, cb_ref, wd_ref, fw_ref, o_ref,
                h_ref, hh_ref, acc_ref, *, final_norm):
    i, f = pl.program_id(0), pl.program_id(1)
    tm, tf = acc_ref.shape[0], wg_ref.shape[1]

    @pl.when(f == 0)
    def _():
        _rms_rows(x_ref, g_ref, h_ref)
        _rms_rows(xh_ref, g_ref, hh_ref)
        hh_ref[...] = jnp.where(i > 0, hh_ref[...], jnp.zeros_like(hh_ref))
        acc_ref[...] = jnp.zeros(acc_ref.shape, F32)

    h = h_ref[...]
    gate = jnp.dot(h, wg_ref[...], preferred_element_type=F32)
    halo = jnp.dot(hh_ref[...], wg_ref[...], preferred_element_type=F32)
    rows = hh_ref.shape[0]
    prev1 = jnp.broadcast_to(halo[rows - 1:rows, :], (tm, tf))
    prev2 = jnp.broadcast_to(halo[rows - 2:rows - 1, :], (tm, tf))
    row = lax.broadcasted_iota(I32, (tm, tf), 0)
    g1 = jnp.where(row < 1, prev1, pltpu.roll(gate, 1, 0))
    g2 = jnp.where(row < 1, prev2, jnp.where(row < 2, prev1, pltpu.roll(gate, 2, 0)))
    cw = cw_ref[...]
    gt = cw[0:1, :] * g2 + cw[1:2, :] * g1 + cw[2:3, :] * gate + cb_ref[...]
    up = jnp.dot(h, wu_ref[...], preferred_element_type=F32)
    act = (gt * _sigmoid(gt) * up).astype(BF16)
    acc_ref[...] += jnp.dot(act, wd_ref[...], preferred_element_type=F32)

    @pl.when(f == pl.num_programs(1) - 1)
    def _():
        if final_norm:
            acc_ref[...] += x_ref[...]
            _rms_rows(acc_ref, fw_ref, o_ref)
        else:
            o_ref[...] = x_ref[...] + acc_ref[...]


def _ffn(x, ln_w, w_gate, w_up, conv_w, conv_b, w_down, final_w, *, final_norm):
    T, D = x.shape
    F = w_gate.shape[1]
    tm = _pick_tile(T, 512, 16)
    tf = _pick_tile(F, 512)
    halo = 8
    vec = lambda n: pl.BlockSpec((1, n), lambda i, f: (0, 0))
    return pl.pallas_call(
        functools.partial(_ffn_kernel, final_norm=final_norm),
        grid=(T // tm, F // tf),
        in_specs=[pl.BlockSpec((tm, D), lambda i, f: (i, 0)),
                  pl.BlockSpec((halo, D), lambda i, f: (jnp.maximum(i * (tm // halo) - 1, 0), 0)),
                  vec(D),
                  pl.BlockSpec((D, tf), lambda i, f: (0, f)),
                  pl.BlockSpec((D, tf), lambda i, f: (0, f)),
                  pl.BlockSpec((CONV_WIDTH, tf), lambda i, f: (0, f)),
                  pl.BlockSpec((1, tf), lambda i, f: (0, f)),
                  pl.BlockSpec((tf, D), lambda i, f: (f, 0)),
                  vec(D)],
        out_specs=pl.BlockSpec((tm, D), lambda i, f: (i, 0)),
        out_shape=jax.ShapeDtypeStruct((T, D), F32),
        scratch_shapes=[pltpu.VMEM((tm, D), BF16), pltpu.VMEM((halo, D), BF16), pltpu.VMEM((tm, D), F32)],
        compiler_params=_cparams("parallel", "arbitrary"),
        name="conv_ffn",
    )(x, x, ln_w.reshape(1, D), w_gate, w_up, conv_w, conv_b.reshape(1, F), w_down, final_w.reshape(1, D))


def _dsa_weight_layout(w, Wm, Wmem):
    n_iq = IDX_HEADS * IDX_DIM
    o = 3 * Wm + n_iq
    ik_iw = w[:, o:o + IDX_DIM + IDX_HEADS]
    mq = w[:, o + IDX_DIM + IDX_HEADS:]
    pad = jnp.zeros((w.shape[0], LANES - IDX_DIM - IDX_HEADS), w.dtype)
    return jnp.concatenate([w[:, :o], mq, ik_iw, pad], axis=1)


def _trunk(x, mem, pos, ln1_w, ln2_w, dsa_w_in, hgrn_w_in, hgrn_lb_logits, hgrn_gnorm_w, mem_norm_w,
           w_mem_kv, w_out, ffn_w_gate, ffn_w_up, ffn_conv_w, ffn_conv_b, ffn_w_down, final_norm_w):
    T, D = x.shape
    Wmem = w_mem_kv.shape[1] // 2
    Wm = D - Wmem
    H = Wm // HEAD_DIM
    depth = ln1_w.shape[0]
    n_sel = min(TOPK_MAX, T // 4)

    mem_kv = _norm_matmul(mem, mem_norm_w, w_mem_kv.astype(BF16), out_dtype=BF16, tn_cap=Wmem, name="mem_kv")
    cos_t, sin_t = _rope_tables(pos.reshape(T, 1))

    for layer in range(depth):
        j = layer // N_MIXERS
        if layer % N_MIXERS == 0:
            w_in = _dsa_weight_layout(dsa_w_in[j], Wm, Wmem).astype(BF16)
            proj = _norm_matmul(x, ln1_w[layer], w_in, out_dtype=F32, tn_cap=1024, name="dsa_proj")
            qh, kh, vh, iq, ika, ikb, iw = _dsa_prep(proj, cos_t, sin_t, H=H)
            bq_idx = min(128, T)
            kc = min(512, T)
            mask = _indexer(iq, ika, ikb, iw, bq=bq_idx, kc=kc, n_sel=n_sel)
            mix = _flash(qh, kh, vh, mask, bq=min(256, T), bk=kc)
            mq_block = (3 * Wm + IDX_HEADS * IDX_DIM) // Wmem
        else:
            proj = _norm_matmul(x, ln1_w[layer], hgrn_w_in[j].astype(BF16), out_dtype=F32, tn_cap=1024,
                                name="hgrn_proj")
            mix = _hgrn(proj, hgrn_lb_logits, hgrn_gnorm_w[j], H=H, layer=layer)
            mq_block = 4 * Wm // Wmem
        x = _outproj(mix, proj, mq_block, mem_kv, w_out[layer].astype(BF16), x)
        x = _ffn(x, ln2_w[layer], ffn_w_gate[layer].astype(BF16), ffn_w_up[layer].astype(BF16),
                 ffn_conv_w[layer], ffn_conv_b[layer], ffn_w_down[layer].astype(BF16), final_norm_w,
                 final_norm=(layer == depth - 1))
    return x


def kernel(x, mem, positions, ln1_w, ln2_w, dsa_w_in, hgrn_w_in, hgrn_lb_logits, hgrn_gnorm_w, mem_norm_w,
           w_mem_kv, w_out, ffn_w_gate, ffn_w_up, ffn_conv_w, ffn_conv_b, ffn_w_down, final_norm_w):
    outs = [_trunk(x[b], mem[b], positions[b], ln1_w, ln2_w, dsa_w_in, hgrn_w_in, hgrn_lb_logits,
                   hgrn_gnorm_w, mem_norm_w, w_mem_kv, w_out, ffn_w_gate, ffn_w_up, ffn_conv_w,
                   ffn_conv_b, ffn_w_down, final_norm_w)
            for b in range(x.shape[0])]
    return jnp.stack(outs, axis=0)
```

```python
import functools

import numpy as np
import jax
import jax.numpy as jnp
from jax import lax
from jax.experimental import pallas as pl
from jax.experimental.pallas import tpu as pltpu

F32 = jnp.float32
BF16 = jnp.bfloat16
I32 = jnp.int32

LANES = 128
HEAD_DIM = 128
IDX_HEADS = 16
IDX_DIM = 64
TOPK_MAX = 256
ROPE_THETA = 500000.0
ROT_FRACTION = 4
N_MIXERS = 2
CONV_WIDTH = 3
NORM_EPS = 1e-6
NEG_INF = -1e30
INT_MIN = -(2 ** 31)

V7X_VMEM_LIMIT_BYTES = 56 * 1024 * 1024

NT_DIMS = (((1,), (1,)), ((), ()))
Q_SCALE_LOG2 = float(HEAD_DIM ** -0.5 * np.log2(np.e))
FLAG_B_FIRST, FLAG_C_FIRST, FLAG_B_LAST, FLAG_C_LAST = 1, 2, 4, 8


def _cparams(*sem):
    return pltpu.CompilerParams(dimension_semantics=sem,
                                vmem_limit_bytes=V7X_VMEM_LIMIT_BYTES)


def _pick_tile(n, cap, mult=LANES):
    best = None
    t = mult
    while t <= min(n, cap):
        if n % t == 0:
            best = t
        t += mult
    assert best is not None, (n, cap, mult)
    return best


def _sigmoid(x):
    return 1.0 / (1.0 + jnp.exp(-x))


def _rms_rows(x_ref, g_ref, out_ref, chunk=64):
    rows = x_ref.shape[0]
    chunk = min(chunk, rows)

    def body(c, carry):
        r = pl.multiple_of(c * chunk, chunk)
        x = x_ref[pl.ds(r, chunk), :].astype(F32)
        ms = jnp.mean(x * x, axis=-1, keepdims=True)
        out_ref[pl.ds(r, chunk), :] = (x * lax.rsqrt(ms + NORM_EPS) * g_ref[...]).astype(out_ref.dtype)
        return carry

    lax.fori_loop(0, rows // chunk, body, 0)


def _rope_lane_constants():
    def pattern(period, rot):
        half = rot // 2
        inv = (np.float32(ROPE_THETA) ** (-(np.arange(half, dtype=np.float32) / np.float32(half)))).astype(np.float32)
        lane = np.arange(LANES) % period
        freq = np.where(lane < rot, inv[lane % half], 0.0).astype(np.float32)
        sign = np.where(lane < half, -1.0, np.where(lane < rot, 1.0, 0.0)).astype(np.float32)
        return freq, sign
    f_att, s_att = pattern(HEAD_DIM, HEAD_DIM // ROT_FRACTION)
    f_idx, s_idx = pattern(IDX_DIM, IDX_DIM // ROT_FRACTION)
    return (np.concatenate([f_att, f_idx])[None, :], np.concatenate([s_att, s_idx])[None, :])


def _rope_tables_kernel(pos_ref, f_ref, s_ref, cos_ref, sin_ref):
    ang = pos_ref[...].astype(F32) * f_ref[...]
    cos_ref[...] = jnp.cos(ang)
    sin_ref[...] = jnp.sin(ang) * s_ref[...]


def _rope_tables(pos_col):
    T = pos_col.shape[0]
    tm = _pick_tile(T, 512, 8)
    freq, sign = _rope_lane_constants()
    row = pl.BlockSpec((tm, 2 * LANES), lambda i: (i, 0))
    const = pl.BlockSpec((1, 2 * LANES), lambda i: (0, 0))
    return pl.pallas_call(
        _rope_tables_kernel,
        grid=(T // tm,),
        in_specs=[pl.BlockSpec((tm, 1), lambda i: (i, 0)), const, const],
        out_specs=[row, row],
        out_shape=[jax.ShapeDtypeStruct((T, 2 * LANES), F32)] * 2,
        compiler_params=_cparams("parallel"),
        name="rope_tables",
    )(pos_col, jnp.asarray(freq), jnp.asarray(sign))


def _norm_matmul_kernel(x_ref, g_ref, w_ref, o_ref, h_ref):
    @pl.when(pl.program_id(1) == 0)
    def _():
        _rms_rows(x_ref, g_ref, h_ref)

    o_ref[...] = jnp.dot(h_ref[...], w_ref[...], preferred_element_type=F32).astype(o_ref.dtype)


def _norm_matmul(x, g, w, *, out_dtype, tm_cap=512, tn_cap=1024, name):
    T, D = x.shape
    N = w.shape[1]
    tm = _pick_tile(T, tm_cap, 8)
    tn = _pick_tile(N, tn_cap)
    return pl.pallas_call(
        _norm_matmul_kernel,
        grid=(T // tm, N // tn),
        in_specs=[pl.BlockSpec((tm, D), lambda i, j: (i, 0)),
                  pl.BlockSpec((1, D), lambda i, j: (0, 0)),
                  pl.BlockSpec((D, tn), lambda i, j: (0, j))],
        out_specs=pl.BlockSpec((tm, tn), lambda i, j: (i, j)),
        out_shape=jax.ShapeDtypeStruct((T, N), out_dtype),
        scratch_shapes=[pltpu.VMEM((tm, D), BF16)],
        compiler_params=_cparams("parallel", "arbitrary"),
        name=name,
    )(x, g.reshape(1, D), w)


def _rope_apply(x, cos, sin, half, period):
    lane = lax.broadcasted_iota(I32, x.shape, 1)
    up = pltpu.roll(x, LANES - half, 1)
    dn = pltpu.roll(x, half, 1)
    partner = jnp.where((lane % period) < half, up, dn)
    return x * cos + partner * sin


def _dsa_prep_kernel(p_ref, cos_ref, sin_ref, q_ref, k_ref, v_ref, iq_ref, ika_ref, ikb_ref, iw_ref, *, H):
    Wm = H * HEAD_DIM
    G = IDX_HEADS * IDX_DIM // LANES
    cos_a, sin_a = cos_ref[:, :LANES], sin_ref[:, :LANES]
    cos_i, sin_i = cos_ref[:, LANES:], sin_ref[:, LANES:]
    half_a = HEAD_DIM // ROT_FRACTION // 2
    half_i = IDX_DIM // ROT_FRACTION // 2
    for h in range(H):
        q = p_ref[:, h * LANES:(h + 1) * LANES]
        q_ref[h] = (_rope_apply(q, cos_a, sin_a, half_a, HEAD_DIM) * Q_SCALE_LOG2).astype(BF16)
        k = p_ref[:, Wm + h * LANES:Wm + (h + 1) * LANES]
        k_ref[h] = _rope_apply(k, cos_a, sin_a, half_a, HEAD_DIM).astype(BF16)
        v_ref[h] = p_ref[:, 2 * Wm + h * LANES:2 * Wm + (h + 1) * LANES].astype(BF16)
    for g in range(G):
        x = p_ref[:, 3 * Wm + g * LANES:3 * Wm + (g + 1) * LANES]
        iq_ref[g] = _rope_apply(x, cos_i, sin_i, half_i, IDX_DIM).astype(BF16)
    tail = p_ref[:, p_ref.shape[1] - LANES:]
    lane = lax.broadcasted_iota(I32, tail.shape, 1)
    ik = jnp.where(lane < IDX_DIM, _rope_apply(tail, cos_i, sin_i, half_i, IDX_DIM), 0.0)
    ika_ref[...] = ik.astype(BF16)
    ikb_ref[...] = pltpu.roll(ik, IDX_DIM, 1).astype(BF16)
    iw = pltpu.roll(tail, LANES - IDX_DIM, 1)
    iw_ref[...] = jnp.where(lane < IDX_HEADS, iw * (IDX_HEADS ** -0.5 * IDX_DIM ** -0.5), 0.0)


def _dsa_prep(proj, cos_t, sin_t, *, H):
    T, N = proj.shape
    G = IDX_HEADS * IDX_DIM // LANES
    tm = _pick_tile(T, 256, 16)
    hm = lambda n: pl.BlockSpec((n, tm, LANES), lambda i: (0, i, 0))
    row = pl.BlockSpec((tm, LANES), lambda i: (i, 0))
    tab = pl.BlockSpec((tm, 2 * LANES), lambda i: (i, 0))
    return pl.pallas_call(
        functools.partial(_dsa_prep_kernel, H=H),
        grid=(T // tm,),
        in_specs=[pl.BlockSpec((tm, N), lambda i: (i, 0)), tab, tab],
        out_specs=[hm(H), hm(H), hm(H), hm(G), row, row, row],
        out_shape=[jax.ShapeDtypeStruct((H, T, LANES), BF16)] * 3
        + [jax.ShapeDtypeStruct((G, T, LANES), BF16)]
        + [jax.ShapeDtypeStruct((T, LANES), BF16)] * 2
        + [jax.ShapeDtypeStruct((T, LANES), F32)],
        compiler_params=_cparams("parallel"),
        name="dsa_prep",
    )(proj, cos_t, sin_t)


def _indexer_kernel(iq_ref, ika_ref, ikb_ref, iw_ref, mask_ref, key_ref, iwb_ref, s_ref, tie_ref,
                    *, bq, kc, n_sel):
    i = pl.program_id(0)
    G = iq_ref.shape[0]
    nkc = key_ref.shape[0]
    reps = kc // LANES
    idx_bits = (nkc * kc - 1).bit_length()
    n_c = ((i + 1) * bq + kc - 1) // kc

    iw = iw_ref[...]
    for h in range(IDX_HEADS):
        iwb_ref[h] = jnp.broadcast_to(iw[:, h:h + 1], (bq, LANES))

    iq = iq_ref[...].reshape(G * bq, LANES)
    qpos = i * bq + lax.broadcasted_iota(I32, (bq, kc), 0)
    lane_k = lax.broadcasted_iota(I32, (bq, kc), 1)

    def score_chunk(c, carry):
        k0 = pl.multiple_of(c * kc, kc)
        score = jnp.zeros((bq, kc), F32)
        for half, ik_ref in enumerate((ika_ref, ikb_ref)):
            s_ref[...] = lax.dot_general(iq, ik_ref[pl.ds(k0, kc), :], NT_DIMS, preferred_element_type=F32)
            for g in range(G):
                w = pltpu.repeat(iwb_ref[2 * g + half], reps, 1)
                score = score + jnp.maximum(s_ref[g * bq:(g + 1) * bq, :], 0.0) * w
        score = jnp.where(k0 + lane_k <= qpos, score, NEG_INF)
        bits = pltpu.bitcast(score, I32)
        key_ref[c] = bits ^ ((bits >> 31) & 0x7FFFFFFF)
        return carry

    lax.fori_loop(0, n_c, score_chunk, 0)

    row_shape = (bq, LANES)

    def row_sum(acc):
        return jnp.broadcast_to(jnp.sum(acc, axis=1, keepdims=True), row_shape)

    def count_ge(cmp):
        def body(c, acc):
            k = key_ref[c]
            for l in range(reps):
                acc = acc + jnp.where(k[:, l * LANES:(l + 1) * LANES] >= cmp, 1.0, 0.0)
            return acc

        return row_sum(lax.fori_loop(0, n_c, body, jnp.zeros(row_shape, F32)))

    def bit_cond(c):
        return jnp.logical_and(c[0] < 32, c[3] > 0.0)

    def bit_step(c):
        b, t, cnt_t, _ = c
        cand = t | jnp.left_shift(jnp.int32(1), 31 - b)
        cnt = count_ge(cand ^ INT_MIN)
        take = cnt >= n_sel
        cnt_t = jnp.where(take, cnt, cnt_t)
        pending = jnp.max(jnp.where(cnt_t == n_sel, 0.0, 1.0))
        return b + 1, jnp.where(take, cand, t), cnt_t, pending

    _, t, cnt_t, _ = lax.while_loop(
        bit_cond, bit_step,
        (jnp.int32(0), jnp.zeros(row_shape, I32), jnp.full(row_shape, float(nkc * kc), F32), jnp.float32(1.0)))
    thr_row = t ^ INT_MIN
    thr = pltpu.repeat(thr_row, reps, 1)

    tie_ref[...] = jnp.full(row_shape, nkc * kc, I32)
    excess = jnp.where(cnt_t > n_sel, 1.0, 0.0)

    @pl.when(jnp.max(excess) > 0.0)
    def _():
        need = n_sel - count_ge(thr_row + 1)
        lane = lax.broadcasted_iota(I32, row_shape, 1)

        def count_tied_below(u):
            def body(c, acc):
                k = key_ref[c]
                for l in range(reps):
                    idx = c * kc + l * LANES + lane
                    tied = k[:, l * LANES:(l + 1) * LANES] == thr_row
                    acc = acc + jnp.where(tied, jnp.where(idx < u, 1.0, 0.0), 0.0)
                return acc

            return row_sum(lax.fori_loop(0, n_c, body, jnp.zeros(row_shape, F32)))

        def idx_step(b, u):
            cand = u | jnp.left_shift(jnp.int32(1), idx_bits - 1 - b)
            return jnp.where(count_tied_below(cand) < need, cand, u)

        u = lax.fori_loop(0, idx_bits, idx_step, jnp.zeros(row_shape, I32))
        tie_ref[...] = jnp.where(excess > 0.0, u, nkc * kc)

    tie = pltpu.repeat(tie_ref[...], reps, 1)

    def emit(c, carry):
        k0 = pl.multiple_of(c * kc, kc)
        kidx = k0 + lane_k
        k = key_ref[c]
        sel = jnp.where(k > thr, 1, jnp.where(k == thr, jnp.where(kidx <= tie, 1, 0), 0))
        mask_ref[0, c] = jnp.where(kidx <= qpos, sel, 0).astype(jnp.int8)
        return carry

    lax.fori_loop(0, n_c, emit, 0)

    def clear(c, carry):
        mask_ref[0, c] = jnp.zeros((bq, kc), jnp.int8)
        return carry

    lax.fori_loop(n_c, nkc, clear, 0)


def _indexer(iq, ika, ikb, iw, *, bq, kc, n_sel):
    G, T, _ = iq.shape
    nq, nkc = T // bq, T // kc
    full = pl.BlockSpec((T, LANES), lambda i: (0, 0))
    row = pl.BlockSpec((bq, LANES), lambda i: (i, 0))
    return pl.pallas_call(
        functools.partial(_indexer_kernel, bq=bq, kc=kc, n_sel=n_sel),
        grid=(nq,),
        in_specs=[pl.BlockSpec((G, bq, LANES), lambda i: (0, i, 0)), full, full, row],
        out_specs=pl.BlockSpec((1, nkc, bq, kc), lambda i: (i, 0, 0, 0)),
        out_shape=jax.ShapeDtypeStruct((nq, nkc, bq, kc), jnp.int8),
        scratch_shapes=[pltpu.VMEM((nkc, bq, kc), I32),
                        pltpu.VMEM((IDX_HEADS, bq, LANES), F32),
                        pltpu.VMEM((G * bq, kc), F32),
                        pltpu.VMEM((bq, LANES), I32)],
        compiler_params=_cparams("parallel"),
        name="dsa_indexer",
    )(iq, ika, ikb, iw)


def _flash_kernel(qa_tab, ka_tab, qb_tab, kb_tab, qc_tab, kc_tab, flag_tab,
                  q_ref, k_ref, v_ref, mask_ref, o_ref,
                  logit_ref, p_ref, alpha_ref, m_ref, l_ref, linv_ref, acc_ref, bias_ref, *, mq_rows):
    s = pl.program_id(0)
    H, bq = q_ref.shape[0], q_ref.shape[1]
    bk = k_ref.shape[1]
    reps = bk // LANES
    flags = flag_tab[s]
    b_first = (flags & FLAG_B_FIRST) != 0
    c_first = (flags & FLAG_C_FIRST) != 0

    @pl.when(s == 0)
    def _():
        logit_ref[...] = jnp.zeros(logit_ref.shape, F32)
        p_ref[...] = jnp.zeros(p_ref.shape, BF16)
        alpha_ref[...] = jnp.zeros(alpha_ref.shape, F32)
        m_ref[...] = jnp.zeros(m_ref.shape, F32)
        l_ref[...] = jnp.zeros(l_ref.shape, F32)
        linv_ref[...] = jnp.zeros(linv_ref.shape, F32)
        acc_ref[...] = jnp.zeros(acc_ref.shape, F32)

    for r in range(bq // mq_rows):
        sel = mask_ref[r, 0].astype(F32)
        bias_ref[r * mq_rows:(r + 1) * mq_rows, :] = (sel - 1.0) * (-NEG_INF)

    for h in range(H):
        acc_prev = jnp.where(c_first, 0.0, acc_ref[h])
        acc_ref[h] = acc_prev * alpha_ref[h] + jnp.dot(p_ref[h], v_ref[h], preferred_element_type=F32)
        logits = logit_ref[h] + bias_ref[...]
        m_prev = jnp.where(b_first, NEG_INF, m_ref[h])
        l_prev = jnp.where(b_first, 0.0, l_ref[h])
        m_next = jnp.maximum(m_prev, jnp.max(logits, axis=1, keepdims=True))
        p = jnp.exp2(logits - pltpu.repeat(m_next, reps, 1))
        alpha = jnp.exp2(m_prev - m_next)
        l_ref[h] = alpha * l_prev + jnp.sum(p, axis=1, keepdims=True)
        m_ref[h] = m_next
        alpha_ref[h] = alpha
        p_ref[h] = p.astype(BF16)
        logit_ref[h] = lax.dot_general(q_ref[h], k_ref[h], NT_DIMS, preferred_element_type=F32)

    @pl.when((flags & FLAG_C_LAST) != 0)
    def _():
        for h in range(H):
            o_ref[:, h * LANES:(h + 1) * LANES] = (acc_ref[h] * linv_ref[h]).astype(o_ref.dtype)

    @pl.when((flags & FLAG_B_LAST) != 0)
    def _():
        for h in range(H):
            linv_ref[h] = 1.0 / l_ref[h]


def _flash(qh, kh, vh, mask, *, bq, bk):
    H, T, _ = qh.shape
    mq_rows = mask.shape[2]
    assert mask.shape[3] == bk and bq % mq_rows == 0
    tiles = [(qi, ki) for qi in range(T // bq) for ki in range(((qi + 1) * bq - 1) // bk + 1)]
    n = len(tiles)
    n_steps = n + 2

    def lagged(lag):
        idx = np.clip(np.arange(n_steps) - lag, 0, n - 1)
        return (np.array([tiles[i][0] for i in idx], np.int32), np.array([tiles[i][1] for i in idx], np.int32))

    (qa, ka), (qb, kb), (qc, kc) = lagged(0), lagged(1), lagged(2)
    step = np.arange(n_steps)
    b_live, c_live = (step >= 1) & (step <= n), step >= 2
    last = lambda q: ((q + 1) * bq - 1) // bk
    flags = (np.where(kb == 0, FLAG_B_FIRST, 0) | np.where(kc == 0, FLAG_C_FIRST, 0)
             | np.where(b_live & (kb == last(qb)), FLAG_B_LAST, 0)
             | np.where(c_live & (kc == last(qc)), FLAG_C_LAST, 0)).astype(np.int32)
    tabs = [jnp.asarray(t) for t in (qa, ka, qb, kb, qc, kc, flags)]
    grid_spec = pltpu.PrefetchScalarGridSpec(
        num_scalar_prefetch=len(tabs),
        grid=(n_steps,),
        in_specs=[pl.BlockSpec((H, bq, LANES), lambda s, qa, ka, qb, kb, qc, kc, fl: (0, qa[s], 0)),
                  pl.BlockSpec((H, bk, LANES), lambda s, qa, ka, qb, kb, qc, kc, fl: (0, ka[s], 0)),
                  pl.BlockSpec((H, bk, LANES), lambda s, qa, ka, qb, kb, qc, kc, fl: (0, kc[s], 0)),
                  pl.BlockSpec((bq // mq_rows, 1, mq_rows, bk),
                               lambda s, qa, ka, qb, kb, qc, kc, fl: (qb[s], kb[s], 0, 0))],
        out_specs=pl.BlockSpec((bq, H * LANES), lambda s, qa, ka, qb, kb, qc, kc, fl: (qc[s], 0)),
        scratch_shapes=[pltpu.VMEM((H, bq, bk), F32),
                        pltpu.VMEM((H, bq, bk), BF16),
                        pltpu.VMEM((H, bq, LANES), F32),
                        pltpu.VMEM((H, bq, LANES), F32),
                        pltpu.VMEM((H, bq, LANES), F32),
                        pltpu.VMEM((H, bq, LANES), F32),
                        pltpu.VMEM((H, bq, LANES), F32),
                        pltpu.VMEM((bq, bk), F32)],
    )
    return pl.pallas_call(
        functools.partial(_flash_kernel, mq_rows=mq_rows),
        grid_spec=grid_spec,
        out_shape=jax.ShapeDtypeStruct((T, H * LANES), BF16),
        compiler_params=_cparams("arbitrary"),
        name="dsa_flash",
    )(*tabs, qh, kh, vh, mask)


def _hgrn_static(C):
    levels = []
    m = C // 2
    while m >= 1:
        levels.append(m)
        m //= 2
    r = np.arange(C)[:, None]
    u = np.arange(C)[None, :]
    mats = [u <= r, u > r]
    masks = []
    for m in levels:
        mid = (r // (2 * m)) * 2 * m + m - 1
        lower = (r % (2 * m)) >= m
        mats.append(np.where(lower, (u > mid) & (u <= r), (u > r) & (u <= mid)))
        masks.append(((r // (2 * m)) == (u // (2 * m))) & lower & ((u % (2 * m)) < m))
    W = np.concatenate(mats, 0).astype(np.float32)
    return np.concatenate([W, W, W], 1), np.stack(masks).astype(np.float32)


def _hgrn_kernel(q_ref, f_ref, i_ref, og_ref, lb_ref, gw_ref, w3_ref, msk_ref, o_ref, st_ref, e_ref,
                 *, C, hp, layer):
    L = msk_ref.shape[0]

    @pl.when(pl.program_id(1) == 0)
    def _():
        st_ref[...] = jnp.zeros(st_ref.shape, F32)

    depth = lb_ref.shape[0]
    rows = [lb_ref[d:d + 1, :] for d in range(depth)]
    mx = functools.reduce(jnp.maximum, rows)
    ex = [jnp.exp(rw - mx) for rw in rows]
    den = functools.reduce(lambda a, b: a + b, ex)
    lb = jnp.zeros_like(mx)
    for d in range(1, layer + 1):
        lb = lb + ex[d] / den

    forget = lb + (1.0 - lb) * _sigmoid(f_ref[...])
    kin = 1.0 - forget
    g = jnp.log(forget)
    g_hi = g.astype(BF16)
    r1 = g - g_hi.astype(F32)
    g_mid = r1.astype(BF16)
    g_lo = (r1 - g_mid.astype(F32)).astype(BF16)
    g3 = jnp.concatenate([g_hi, g_mid, g_lo], axis=0)
    e_ref[...] = jnp.dot(w3_ref[...], g3, preferred_element_type=F32)

    q = q_ref[...]
    qs = q * _sigmoid(q)
    og = og_ref[...]
    gate = og * _sigmoid(og)
    eye = lax.broadcasted_iota(I32, (C, C), 0) == lax.broadcasted_iota(I32, (C, C), 1)

    for hh in range(hp):
        sl = slice(hh * LANES, (hh + 1) * LANES)
        qh, kh, vh = qs[:, sl], kin[:, sl], i_ref[:, sl]
        st = st_ref[hh]
        o = lax.dot_general((qh * jnp.exp(e_ref[0:C, sl])).astype(BF16), st.astype(BF16), NT_DIMS,
                            preferred_element_type=F32)
        a = jnp.where(eye, jnp.sum(qh * kh, axis=1, keepdims=True), 0.0)
        for l in range(L):
            el = jnp.exp(e_ref[(2 + l) * C:(3 + l) * C, sl])
            a = a + msk_ref[l] * lax.dot_general((qh * el).astype(BF16), (kh * el).astype(BF16), NT_DIMS,
                                                 preferred_element_type=F32)
        vb = vh.astype(BF16)
        o = o + jnp.dot(a.astype(BF16), vb, preferred_element_type=F32)
        kd = (kh * jnp.exp(e_ref[C:2 * C, sl])).astype(BF16)
        st_ref[hh] = st * jnp.exp(e_ref[C - 1:C, sl]) + jnp.dot(vh.T.astype(BF16), kd, preferred_element_type=F32)
        o = o * lax.rsqrt(jnp.mean(o * o, axis=1, keepdims=True) + NORM_EPS) * gw_ref[...]
        o_ref[:, sl] = (o * gate[:, sl]).astype(o_ref.dtype)


def _hgrn(proj, lb_logits, gnorm_w, *, H, layer, C=128, hp=2):
    T = proj.shape[0]
    hp = hp if H % hp == 0 else 1
    wb = hp * LANES
    nb = H // hp
    w3, masks = _hgrn_static(C)
    L = masks.shape[0]
    col = lambda k: pl.BlockSpec((C, wb), lambda hb, c: (c, hb + k * nb))
    return pl.pallas_call(
        functools.partial(_hgrn_kernel, C=C, hp=hp, layer=layer),
        grid=(nb, T // C),
        in_specs=[col(0), col(1), col(2), col(3),
                  pl.BlockSpec((lb_logits.shape[0], wb), lambda hb, c: (0, hb)),
                  pl.BlockSpec((1, LANES), lambda hb, c: (0, 0)),
                  pl.BlockSpec(w3.shape, lambda hb, c: (0, 0)),
                  pl.BlockSpec(masks.shape, lambda hb, c: (0, 0, 0))],
        out_specs=pl.BlockSpec((C, wb), lambda hb, c: (c, hb)),
        out_shape=jax.ShapeDtypeStruct((T, H * LANES), BF16),
        scratch_shapes=[pltpu.VMEM((hp, LANES, LANES), F32),
                        pltpu.VMEM(((2 + L) * C, wb), F32)],
        compiler_params=_cparams("parallel", "arbitrary"),
        name="hgrn2",
    )(proj, proj, proj, proj, lb_logits, gnorm_w.reshape(1, LANES), jnp.asarray(w3, BF16), jnp.asarray(masks))


def _outproj_kernel(mix_ref, mq_ref, mk_ref, mv_ref, w_ref, x_ref, o_ref, cat_ref):
    Wm = mix_ref.shape[1]
    scale = HEAD_DIM ** -0.5
    cat_ref[:, :Wm] = mix_ref[...]
    for h in range(mq_ref.shape[1] // LANES):
        sl = slice(h * LANES, (h + 1) * LANES)
        logits = lax.dot_general(mq_ref[:, sl].astype(BF16), mk_ref[:, sl], NT_DIMS,
                                 preferred_element_type=F32) * scale
        p = jnp.exp(logits - jnp.max(logits, axis=1, keepdims=True))
        den = jnp.sum(p, axis=1, keepdims=True)
        o = jnp.dot(p.astype(BF16), mv_ref[:, sl], preferred_element_type=F32) / den
        cat_ref[:, Wm + h * LANES:Wm + (h + 1) * LANES] = o.astype(BF16)
    o_ref[...] = x_ref[...] + jnp.dot(cat_ref[...], w_ref[...], preferred_element_type=F32)


def _outproj(mix, proj, mq_block, mem_kv, w_out, x):
    T, D = x.shape
    Wm = mix.shape[1]
    Wmem = D - Wm
    M = mem_kv.shape[0]
    tm = _pick_tile(T, 512, 16)
    return pl.pallas_call(
        _outproj_kernel,
        grid=(T // tm,),
        in_specs=[pl.BlockSpec((tm, Wm), lambda i: (i, 0)),
                  pl.BlockSpec((tm, Wmem), lambda i: (i, mq_block)),
                  pl.BlockSpec((M, Wmem), lambda i: (0, 0)),
                  pl.BlockSpec((M, Wmem), lambda i: (0, 1)),
                  pl.BlockSpec((D, D), lambda i: (0, 0)),
                  pl.BlockSpec((tm, D), lambda i: (i, 0))],
        out_specs=pl.BlockSpec((tm, D), lambda i: (i, 0)),
        out_shape=jax.ShapeDtypeStruct((T, D), F32),
        scratch_shapes=[pltpu.VMEM((tm, D), BF16)],
        compiler_params=_cparams("parallel"),
        name="outproj",
    )(mix, proj, mem_kv, mem_kv, w_out, x)


def _ffn_kernel(x_ref, xh_ref, g_ref, wg_ref, wu_ref, cw_ref, cb_ref, wd_ref, fw_ref, o_ref,
                h_ref, hh_ref, acc_ref, *, final_norm):
    i, f = pl.program_id(0), pl.program_id(1)
    tm, tf = acc_ref.shape[0], wg_ref.shape[1]

    @pl.when(f == 0)
    def _():
        _rms_rows(x_ref, g_ref, h_ref)
        _rms_rows(xh_ref, g_ref, hh_ref)
        hh_ref[...] = jnp.where(i > 0, hh_ref[...], jnp.zeros_like(hh_ref))
        acc_ref[...] = jnp.zeros(acc_ref.shape, F32)

    h = h_ref[...]
    gate = jnp.dot(h, wg_ref[...], preferred_element_type=F32)
    halo = jnp.dot(hh_ref[...], wg_ref[...], preferred_element_type=F32)
    rows = hh_ref.shape[0]
    prev1 = jnp.broadcast_to(halo[rows - 1:rows, :], (tm, tf))
    prev2 = jnp.broadcast_to(halo[rows - 2:rows - 1, :], (tm, tf))
    row = lax.broadcasted_iota(I32, (tm, tf), 0)
    g1 = jnp.where(row < 1, prev1, pltpu.roll(gate, 1, 0))
    g2 = jnp.where(row < 1, prev2, jnp.where(row < 2, prev1, pltpu.roll(gate, 2, 0)))
    cw = cw_ref[...]
    gt = cw[0:1, :] * g2 + cw[1:2, :] * g1 + cw[2:3, :] * gate + cb_ref[...]
    up = jnp.dot(h, wu_ref[...], preferred_element_type=F32)
    act = (gt * _sigmoid(gt) * up).astype(BF16)
    acc_ref[...] += jnp.dot(act, wd_ref[...], preferred_element_type=F32)

    @pl.when(f == pl.num_programs(1) - 1)
    def _():
        if final_norm:
            acc_ref[...] += x_ref[...]
            _rms_rows(acc_ref, fw_ref, o_ref)
        else:
            o_ref[...] = x_ref[...] + acc_ref[...]


def _ffn(x, ln_w, w_gate, w_up, conv_w, conv_b, w_down, final_w, *, final_norm):
    T, D = x.shape
    F = w_gate.shape[1]
    tm = _pick_tile(T, 512, 16)
    tf = _pick_tile(F, 512)
    halo = 8
    vec = lambda n: pl.BlockSpec((1, n), lambda i, f: (0, 0))
    return pl.pallas_call(
        functools.partial(_ffn_kernel, final_norm=final_norm),
        grid=(T // tm, F // tf),
        in_specs=[pl.BlockSpec((tm, D), lambda i, f: (i, 0)),
                  pl.BlockSpec((halo, D), lambda i, f: (jnp.maximum(i * (tm // halo) - 1, 0), 0)),
                  vec(D),
                  pl.BlockSpec((D, tf), lambda i, f: (0, f)),
                  pl.BlockSpec((D, tf), lambda i, f: (0, f)),
                  pl.BlockSpec((CONV_WIDTH, tf), lambda i, f: (0, f)),
                  pl.BlockSpec((1, tf), lambda i, f: (0, f)),
                  pl.BlockSpec((tf, D), lambda i, f: (f, 0)),
                  vec(D)],
        out_specs=pl.BlockSpec((tm, D), lambda i, f: (i, 0)),
        out_shape=jax.ShapeDtypeStruct((T, D), F32),
        scratch_shapes=[pltpu.VMEM((tm, D), BF16), pltpu.VMEM((halo, D), BF16), pltpu.VMEM((tm, D), F32)],
        compiler_params=_cparams("parallel", "arbitrary"),
        name="conv_ffn",
    )(x, x, ln_w.reshape(1, D), w_gate, w_up, conv_w, conv_b.reshape(1, F), w_down, final_w.reshape(1, D))


def _dsa_weight_layout(w, Wm, Wmem):
    n_iq = IDX_HEADS * IDX_DIM
    o = 3 * Wm + n_iq
    ik_iw = w[:, o:o + IDX_DIM + IDX_HEADS]
    mq = w[:, o + IDX_DIM + IDX_HEADS:]
    pad = jnp.zeros((w.shape[0], LANES - IDX_DIM - IDX_HEADS), w.dtype)
    return jnp.concatenate([w[:, :o], mq, ik_iw, pad], axis=1)


def _trunk(x, mem, pos, ln1_w, ln2_w, dsa_w_in, hgrn_w_in, hgrn_lb_logits, hgrn_gnorm_w, mem_norm_w,
           w_mem_kv, w_out, ffn_w_gate, ffn_w_up, ffn_conv_w, ffn_conv_b, ffn_w_down, final_norm_w):
    T, D = x.shape
    Wmem = w_mem_kv.shape[1] // 2
    Wm = D - Wmem
    H = Wm // HEAD_DIM
    depth = ln1_w.shape[0]
    n_sel = min(TOPK_MAX, T // 4)

    mem_kv = _norm_matmul(mem, mem_norm_w, w_mem_kv.astype(BF16), out_dtype=BF16, tn_cap=Wmem, name="mem_kv")
    cos_t, sin_t = _rope_tables(pos.reshape(T, 1))

    for layer in range(depth):
        j = layer // N_MIXERS
        if layer % N_MIXERS == 0:
            w_in = _dsa_weight_layout(dsa_w_in[j], Wm, Wmem).astype(BF16)
            proj = _norm_matmul(x, ln1_w[layer], w_in, out_dtype=F32, tn_cap=1024, name="dsa_proj")
            qh, kh, vh, iq, ika, ikb, iw = _dsa_prep(proj, cos_t, sin_t, H=H)
            bq_idx = min(128, T)
            kc = min(512, T)
            mask = _indexer(iq, ika, ikb, iw, bq=bq_idx, kc=kc, n_sel=n_sel)
            mix = _flash(qh, kh, vh, mask, bq=min(256, T), bk=kc)
            mq_block = (3 * Wm + IDX_HEADS * IDX_DIM) // Wmem
        else:
            proj = _norm_matmul(x, ln1_w[layer], hgrn_w_in[j].astype(BF16), out_dtype=F32, tn_cap=1024,
                                name="hgrn_proj")
            mix = _hgrn(proj, hgrn_lb_logits, hgrn_gnorm_w[j], H=H, layer=layer)
            mq_block = 4 * Wm // Wmem
        x = _outproj(mix, proj, mq_block, mem_kv, w_out[layer].astype(BF16), x)
        x = _ffn(x, ln2_w[layer], ffn_w_gate[layer].astype(BF16), ffn_w_up[layer].astype(BF16),
                 ffn_conv_w[layer], ffn_conv_b[layer], ffn_w_down[layer].astype(BF16), final_norm_w,
                 final_norm=(layer == depth - 1))
    return x


def kernel(x, mem, positions, ln1_w, ln2_w, dsa_w_in, hgrn_w_in, hgrn_lb_logits, hgrn_gnorm_w, mem_norm_w,
           w_mem_kv, w_out, ffn_w_gate, ffn_w_up, ffn_conv_w, ffn_conv_b, ffn_w_down, final_norm_w):
    outs = [_trunk(x[b], mem[b], positions[b], ln1_w, ln2_w, dsa_w_in, hgrn_w_in, hgrn_lb_logits,
                   hgrn_gnorm_w, mem_norm_w, w_mem_kv, w_out, ffn_w_gate, ffn_w_up, ffn_conv_w,
                   ffn_conv_b, ffn_w_down, final_norm_w)
            for b in range(x.shape[0])]
    return jnp.stack(outs, axis=0)
```

```python
import functools

import numpy as np
import jax
import jax.numpy as jnp
from jax import lax
from jax.experimental import pallas as pl
from jax.experimental.pallas import tpu as pltpu

F32 = jnp.float32
BF16 = jnp.bfloat16
I32 = jnp.int32
I16 = jnp.int16
HALF16 = 1 << 15

LANES = 128
HEAD_DIM = 128
IDX_HEADS = 16
IDX_DIM = 64
TOPK_MAX = 256
ROPE_THETA = 500000.0
ROT_FRACTION = 4
N_MIXERS = 2
CONV_WIDTH = 3
NORM_EPS = 1e-6
NEG_INF = -1e30
INT_MIN = -(2 ** 31)

V7X_VMEM_LIMIT_BYTES = 56 * 1024 * 1024

NT_DIMS = (((1,), (1,)), ((), ()))
Q_SCALE_LOG2 = float(HEAD_DIM ** -0.5 * np.log2(np.e))
FLAG_B_FIRST, FLAG_C_FIRST, FLAG_B_LAST, FLAG_C_LAST = 1, 2, 4, 8


def _cparams(*sem):
    return pltpu.CompilerParams(dimension_semantics=sem,
                                vmem_limit_bytes=V7X_VMEM_LIMIT_BYTES)


def _pick_tile(n, cap, mult=LANES):
    best = None
    t = mult
    while t <= min(n, cap):
        if n % t == 0:
            best = t
        t += mult
    assert best is not None, (n, cap, mult)
    return best


def _sigmoid(x):
    return 1.0 / (1.0 + jnp.exp(-x))


def _rms_rows(x_ref, g_ref, out_ref, chunk=64):
    rows = x_ref.shape[0]
    chunk = min(chunk, rows)

    def body(c, carry):
        r = pl.multiple_of(c * chunk, chunk)
        x = x_ref[pl.ds(r, chunk), :].astype(F32)
        ms = jnp.mean(x * x, axis=-1, keepdims=True)
        out_ref[pl.ds(r, chunk), :] = (x * lax.rsqrt(ms + NORM_EPS) * g_ref[...]).astype(out_ref.dtype)
        return carry

    lax.fori_loop(0, rows // chunk, body, 0)


def _rope_lane_constants():
    def pattern(period, rot):
        half = rot // 2
        inv = (np.float32(ROPE_THETA) ** (-(np.arange(half, dtype=np.float32) / np.float32(half)))).astype(np.float32)
        lane = np.arange(LANES) % period
        freq = np.where(lane < rot, inv[lane % half], 0.0).astype(np.float32)
        sign = np.where(lane < half, -1.0, np.where(lane < rot, 1.0, 0.0)).astype(np.float32)
        return freq, sign
    f_att, s_att = pattern(HEAD_DIM, HEAD_DIM // ROT_FRACTION)
    f_idx, s_idx = pattern(IDX_DIM, IDX_DIM // ROT_FRACTION)
    return (np.concatenate([f_att, f_idx])[None, :], np.concatenate([s_att, s_idx])[None, :])


def _rope_tables_kernel(pos_ref, f_ref, s_ref, cos_ref, sin_ref):
    ang = pos_ref[...].astype(F32) * f_ref[...]
    cos_ref[...] = jnp.cos(ang)
    sin_ref[...] = jnp.sin(ang) * s_ref[...]


def _rope_tables(pos_col):
    T = pos_col.shape[0]
    tm = _pick_tile(T, 512, 8)
    freq, sign = _rope_lane_constants()
    row = pl.BlockSpec((tm, 2 * LANES), lambda i: (i, 0))
    const = pl.BlockSpec((1, 2 * LANES), lambda i: (0, 0))
    return pl.pallas_call(
        _rope_tables_kernel,
        grid=(T // tm,),
        in_specs=[pl.BlockSpec((tm, 1), lambda i: (i, 0)), const, const],
        out_specs=[row, row],
        out_shape=[jax.ShapeDtypeStruct((T, 2 * LANES), F32)] * 2,
        compiler_params=_cparams("parallel"),
        name="rope_tables",
    )(pos_col, jnp.asarray(freq), jnp.asarray(sign))


def _norm_matmul_kernel(x_ref, g_ref, w_ref, o_ref, h_ref):
    @pl.when(pl.program_id(1) == 0)
    def _():
        _rms_rows(x_ref, g_ref, h_ref)

    o_ref[...] = jnp.dot(h_ref[...], w_ref[...], preferred_element_type=F32).astype(o_ref.dtype)


def _norm_matmul(x, g, w, *, out_dtype, tm_cap=1024, tn_cap=1024, name):
    T, D = x.shape
    N = w.shape[1]
    tm = _pick_tile(T, tm_cap, 8)
    tn = _pick_tile(N, tn_cap)
    return pl.pallas_call(
        _norm_matmul_kernel,
        grid=(T // tm, N // tn),
        in_specs=[pl.BlockSpec((tm, D), lambda i, j: (i, 0)),
                  pl.BlockSpec((1, D), lambda i, j: (0, 0)),
                  pl.BlockSpec((D, tn), lambda i, j: (0, j))],
        out_specs=pl.BlockSpec((tm, tn), lambda i, j: (i, j)),
        out_shape=jax.ShapeDtypeStruct((T, N), out_dtype),
        scratch_shapes=[pltpu.VMEM((tm, D), BF16)],
        compiler_params=_cparams("parallel", "arbitrary"),
        name=name,
    )(x, g.reshape(1, D), w)


def _rope_apply(x, cos, sin, half, period):
    lane = lax.broadcasted_iota(I32, x.shape, 1)
    up = pltpu.roll(x, LANES - half, 1)
    dn = pltpu.roll(x, half, 1)
    partner = jnp.where((lane % period) < half, up, dn)
    return x * cos + partner * sin


def _dsa_prep_kernel(p_ref, cos_ref, sin_ref, q_ref, k_ref, v_ref, iq_ref, ika_ref, ikb_ref, iw_ref, *, H):
    Wm = H * HEAD_DIM
    G = IDX_HEADS * IDX_DIM // LANES
    cos_a, sin_a = cos_ref[:, :LANES], sin_ref[:, :LANES]
    cos_i, sin_i = cos_ref[:, LANES:], sin_ref[:, LANES:]
    half_a = HEAD_DIM // ROT_FRACTION // 2
    half_i = IDX_DIM // ROT_FRACTION // 2
    for h in range(H):
        q = p_ref[:, h * LANES:(h + 1) * LANES]
        q_ref[h] = (_rope_apply(q, cos_a, sin_a, half_a, HEAD_DIM) * Q_SCALE_LOG2).astype(BF16)
        k = p_ref[:, Wm + h * LANES:Wm + (h + 1) * LANES]
        k_ref[h] = _rope_apply(k, cos_a, sin_a, half_a, HEAD_DIM).astype(BF16)
        v_ref[h] = p_ref[:, 2 * Wm + h * LANES:2 * Wm + (h + 1) * LANES].astype(BF16)
    for g in range(G):
        x = p_ref[:, 3 * Wm + g * LANES:3 * Wm + (g + 1) * LANES]
        iq_ref[g] = _rope_apply(x, cos_i, sin_i, half_i, IDX_DIM).astype(BF16)
    tail = p_ref[:, p_ref.shape[1] - LANES:]
    lane = lax.broadcasted_iota(I32, tail.shape, 1)
    ik = jnp.where(lane < IDX_DIM, _rope_apply(tail, cos_i, sin_i, half_i, IDX_DIM), 0.0)
    ika_ref[...] = ik.astype(BF16)
    ikb_ref[...] = pltpu.roll(ik, IDX_DIM, 1).astype(BF16)
    iw = pltpu.roll(tail, LANES - IDX_DIM, 1)
    iw_ref[...] = jnp.where(lane < IDX_HEADS, iw * (IDX_HEADS ** -0.5 * IDX_DIM ** -0.5), 0.0)


def _dsa_prep(proj, cos_t, sin_t, *, H):
    T, N = proj.shape
    G = IDX_HEADS * IDX_DIM // LANES
    tm = _pick_tile(T, 256, 16)
    hm = lambda n: pl.BlockSpec((n, tm, LANES), lambda i: (0, i, 0))
    row = pl.BlockSpec((tm, LANES), lambda i: (i, 0))
    tab = pl.BlockSpec((tm, 2 * LANES), lambda i: (i, 0))
    return pl.pallas_call(
        functools.partial(_dsa_prep_kernel, H=H),
        grid=(T // tm,),
        in_specs=[pl.BlockSpec((tm, N), lambda i: (i, 0)), tab, tab],
        out_specs=[hm(H), hm(H), hm(H), hm(G), row, row, row],
        out_shape=[jax.ShapeDtypeStruct((H, T, LANES), BF16)] * 3
        + [jax.ShapeDtypeStruct((G, T, LANES), BF16)]
        + [jax.ShapeDtypeStruct((T, LANES), BF16)] * 2
        + [jax.ShapeDtypeStruct((T, LANES), F32)],
        compiler_params=_cparams("parallel"),
        name="dsa_prep",
    )(proj, cos_t, sin_t)


def _indexer_kernel(iq_ref, ika_ref, ikb_ref, iw_ref, mask_ref, hi_ref, lo_ref, iwb_ref, s_ref, tie_ref,
                    *, bq, kc, n_sel):
    i = pl.program_id(0)
    G = iq_ref.shape[0]
    nkc = hi_ref.shape[0]
    reps = kc // LANES
    idx_bits = (nkc * kc - 1).bit_length()
    n_c = ((i + 1) * bq + kc - 1) // kc

    iw = iw_ref[...]
    for h in range(IDX_HEADS):
        iwb_ref[h] = jnp.broadcast_to(iw[:, h:h + 1], (bq, LANES))

    iq = iq_ref[...].reshape(G * bq, LANES)
    qpos = i * bq + lax.broadcasted_iota(I32, (bq, kc), 0)
    lane_k = lax.broadcasted_iota(I32, (bq, kc), 1)

    def score_chunk(c, carry):
        k0 = pl.multiple_of(c * kc, kc)
        score = jnp.zeros((bq, kc), F32)
        for half, ik_ref in enumerate((ika_ref, ikb_ref)):
            s_ref[...] = lax.dot_general(iq, ik_ref[pl.ds(k0, kc), :], NT_DIMS, preferred_element_type=F32)
            for g in range(G):
                w = pltpu.repeat(iwb_ref[2 * g + half], reps, 1)
                score = score + jnp.maximum(s_ref[g * bq:(g + 1) * bq, :], 0.0) * w
        score = jnp.where(k0 + lane_k <= qpos, score, NEG_INF)
        bits = pltpu.bitcast(score, I32)
        key = bits ^ ((bits >> 31) & 0x7FFFFFFF)
        hi_ref[c] = (key >> 16).astype(I16)
        lo_ref[c] = ((key & 0xFFFF) - HALF16).astype(I16)
        return carry

    lax.fori_loop(0, n_c, score_chunk, 0)

    row_shape = (bq, LANES)
    one, zero = jnp.int16(1), jnp.int16(0)

    def count_rows(indicator):
        def body(c, acc):
            for l in range(reps):
                acc = acc + indicator(c, l)
            return acc

        acc = lax.fori_loop(0, n_c, body, jnp.zeros(row_shape, I16))
        return jnp.broadcast_to(jnp.sum(acc.astype(F32), axis=1, keepdims=True), row_shape)

    def piece(ref, c, l):
        return ref[c, :, l * LANES:(l + 1) * LANES]

    def kth_largest(ref, k_rows, cnt0):
        def cond(c):
            return jnp.logical_and(c[0] < 16, c[3] > 0.0)

        def step(c):
            b, t, cnt_t, _ = c
            cand = t | jnp.left_shift(jnp.int32(1), 15 - b)
            cmp = (cand - HALF16).astype(I16)
            cnt = count_rows(lambda c_, l: jnp.where(piece(ref, c_, l) >= cmp, one, zero))
            take = cnt >= k_rows
            cnt_t = jnp.where(take, cnt, cnt_t)
            pending = jnp.max(jnp.where(cnt_t == k_rows, 0.0, 1.0))
            return b + 1, jnp.where(take, cand, t), cnt_t, pending

        pending0 = jnp.max(jnp.where(cnt0 == k_rows, 0.0, 1.0))
        _, t, cnt_t, _ = lax.while_loop(cond, step, (jnp.int32(0), jnp.zeros(row_shape, I32), cnt0, pending0))
        return (t - HALF16).astype(I16), cnt_t

    k_all = jnp.full(row_shape, float(n_sel), F32)
    th, cnt_ge_hi = kth_largest(hi_ref, k_all, jnp.full(row_shape, float(nkc * kc), F32))
    cnt_gt_hi = count_rows(lambda c_, l: jnp.where(piece(hi_ref, c_, l) > th, one, zero))

    def keep_group(c, carry):
        for l in range(reps):
            sl = slice(l * LANES, (l + 1) * LANES)
            lo_ref[c, :, sl] = jnp.where(hi_ref[c, :, sl] == th, lo_ref[c, :, sl], jnp.int16(-HALF16))
        return carry

    lax.fori_loop(0, n_c, keep_group, 0)
    tl, cnt_ge_lo = kth_largest(lo_ref, k_all - cnt_gt_hi, cnt_ge_hi - cnt_gt_hi)

    def tied(c, l):
        return jnp.logical_and(piece(lo_ref, c, l) == tl, piece(hi_ref, c, l) == th)

    tie_ref[...] = jnp.full(row_shape, nkc * kc - 1, I16)
    excess = jnp.where(cnt_gt_hi + cnt_ge_lo > n_sel, 1.0, 0.0)
    lane32 = lax.broadcasted_iota(I32, row_shape, 1)

    def key_index(c, l):
        return (lane32 + (c * kc + l * LANES)).astype(I16)

    @pl.when(jnp.max(excess) > 0.0)
    def _():
        cnt_gt = cnt_gt_hi + count_rows(lambda c_, l: jnp.where(piece(lo_ref, c_, l) > tl, one, zero))
        need = n_sel - cnt_gt

        def idx_step(b, u):
            cand = u | jnp.left_shift(jnp.int32(1), idx_bits - 1 - b)
            cand16 = cand.astype(I16)
            below = count_rows(lambda c_, l: jnp.where(
                tied(c_, l), jnp.where(key_index(c_, l) < cand16, one, zero), zero))
            return jnp.where(below < need, cand, u)

        u = lax.fori_loop(0, idx_bits, idx_step, jnp.zeros(row_shape, I32))
        tie_ref[...] = jnp.where(excess > 0.0, u, nkc * kc - 1).astype(I16)

    tie = tie_ref[...]
    qpos16 = (i * bq + lax.broadcasted_iota(I32, row_shape, 0)).astype(I16)

    def emit(c, carry):
        for l in range(reps):
            kidx = key_index(c, l)
            hi, lo = piece(hi_ref, c, l), piece(lo_ref, c, l)
            in_group = jnp.where(lo > tl, one, jnp.where(lo == tl, jnp.where(kidx <= tie, one, zero), zero))
            sel = jnp.where(hi > th, one, jnp.where(hi == th, in_group, zero))
            mask_ref[0, c, :, l * LANES:(l + 1) * LANES] = jnp.where(kidx <= qpos16, sel, zero).astype(jnp.int8)
        return carry

    lax.fori_loop(0, n_c, emit, 0)

    def clear(c, carry):
        mask_ref[0, c] = jnp.zeros((bq, kc), jnp.int8)
        return carry

    lax.fori_loop(n_c, nkc, clear, 0)


def _indexer(iq, ika, ikb, iw, *, bq, kc, n_sel):
    G, T, _ = iq.shape
    assert T <= HALF16, "key indices and per-lane counts are held in int16"
    nq, nkc = T // bq, T // kc
    full = pl.BlockSpec((T, LANES), lambda i: (0, 0))
    row = pl.BlockSpec((bq, LANES), lambda i: (i, 0))
    return pl.pallas_call(
        functools.partial(_indexer_kernel, bq=bq, kc=kc, n_sel=n_sel),
        grid=(nq,),
        in_specs=[pl.BlockSpec((G, bq, LANES), lambda i: (0, i, 0)), full, full, row],
        out_specs=pl.BlockSpec((1, nkc, bq, kc), lambda i: (i, 0, 0, 0)),
        out_shape=jax.ShapeDtypeStruct((nq, nkc, bq, kc), jnp.int8),
        scratch_shapes=[pltpu.VMEM((nkc, bq, kc), I16),
                        pltpu.VMEM((nkc, bq, kc), I16),
                        pltpu.VMEM((IDX_HEADS, bq, LANES), F32),
                        pltpu.VMEM((G * bq, kc), F32),
                        pltpu.VMEM((bq, LANES), I16)],
        compiler_params=_cparams("parallel"),
        name="dsa_indexer",
    )(iq, ika, ikb, iw)


def _flash_kernel(qa_tab, ka_tab, qb_tab, kb_tab, qc_tab, kc_tab, flag_tab,
                  q_ref, k_ref, v_ref, mask_ref, o_ref,
                  logit_ref, p_ref, alpha_ref, m_ref, l_ref, linv_ref, acc_ref, bias_ref, *, mq_rows):
    s = pl.program_id(0)
    H, bq = q_ref.shape[0], q_ref.shape[1]
    bk = k_ref.shape[1]
    reps = bk // LANES
    flags = flag_tab[s]
    b_first = (flags & FLAG_B_FIRST) != 0
    c_first = (flags & FLAG_C_FIRST) != 0

    @pl.when(s == 0)
    def _():
        logit_ref[...] = jnp.zeros(logit_ref.shape, F32)
        p_ref[...] = jnp.zeros(p_ref.shape, BF16)
        alpha_ref[...] = jnp.zeros(alpha_ref.shape, F32)
        m_ref[...] = jnp.zeros(m_ref.shape, F32)
        l_ref[...] = jnp.zeros(l_ref.shape, F32)
        linv_ref[...] = jnp.zeros(linv_ref.shape, F32)
        acc_ref[...] = jnp.zeros(acc_ref.shape, F32)

    for r in range(bq // mq_rows):
        sel = mask_ref[r, 0].astype(F32)
        bias_ref[r * mq_rows:(r + 1) * mq_rows, :] = (sel - 1.0) * (-NEG_INF)

    for h in range(H):
        acc_prev = jnp.where(c_first, 0.0, acc_ref[h])
        acc_ref[h] = acc_prev * alpha_ref[h] + jnp.dot(p_ref[h], v_ref[h], preferred_element_type=F32)
        m_prev = jnp.where(b_first, NEG_INF, m_ref[h])
        m_next = jnp.maximum(m_prev, jnp.max(logit_ref[h], axis=1, keepdims=True))
        m_ref[h] = m_next
        alpha_ref[h] = jnp.exp2(m_prev - m_next)

    for h in range(H):
        p = jnp.exp2(logit_ref[h] - pltpu.repeat(m_ref[h], reps, 1))
        l_prev = jnp.where(b_first, 0.0, l_ref[h])
        l_ref[h] = alpha_ref[h] * l_prev + jnp.sum(p, axis=1, keepdims=True)
        p_ref[h] = p.astype(BF16)
        logit_ref[h] = lax.dot_general(q_ref[h], k_ref[h], NT_DIMS, preferred_element_type=F32) + bias_ref[...]

    @pl.when((flags & FLAG_C_LAST) != 0)
    def _():
        for h in range(H):
            o_ref[:, h * LANES:(h + 1) * LANES] = (acc_ref[h] * linv_ref[h]).astype(o_ref.dtype)

    @pl.when((flags & FLAG_B_LAST) != 0)
    def _():
        for h in range(H):
            linv_ref[h] = 1.0 / l_ref[h]


def _flash(qh, kh, vh, mask, *, bq, bk):
    H, T, _ = qh.shape
    mq_rows = mask.shape[2]
    assert mask.shape[3] == bk and bq % mq_rows == 0
    tiles = [(qi, ki) for qi in range(T // bq) for ki in range(((qi + 1) * bq - 1) // bk + 1)]
    n = len(tiles)
    n_steps = n + 2

    def lagged(lag):
        idx = np.clip(np.arange(n_steps) - lag, 0, n - 1)
        return (np.array([tiles[i][0] for i in idx], np.int32), np.array([tiles[i][1] for i in idx], np.int32))

    (qa, ka), (qb, kb), (qc, kc) = lagged(0), lagged(1), lagged(2)
    step = np.arange(n_steps)
    b_live, c_live = (step >= 1) & (step <= n), step >= 2
    last = lambda q: ((q + 1) * bq - 1) // bk
    flags = (np.where(kb == 0, FLAG_B_FIRST, 0) | np.where(kc == 0, FLAG_C_FIRST, 0)
             | np.where(b_live & (kb == last(qb)), FLAG_B_LAST, 0)
             | np.where(c_live & (kc == last(qc)), FLAG_C_LAST, 0)).astype(np.int32)
    tabs = [jnp.asarray(t) for t in (qa, ka, qb, kb, qc, kc, flags)]
    grid_spec = pltpu.PrefetchScalarGridSpec(
        num_scalar_prefetch=len(tabs),
        grid=(n_steps,),
        in_specs=[pl.BlockSpec((H, bq, LANES), lambda s, qa, ka, qb, kb, qc, kc, fl: (0, qa[s], 0)),
                  pl.BlockSpec((H, bk, LANES), lambda s, qa, ka, qb, kb, qc, kc, fl: (0, ka[s], 0)),
                  pl.BlockSpec((H, bk, LANES), lambda s, qa, ka, qb, kb, qc, kc, fl: (0, kc[s], 0)),
                  pl.BlockSpec((bq // mq_rows, 1, mq_rows, bk),
                               lambda s, qa, ka, qb, kb, qc, kc, fl: (qa[s], ka[s], 0, 0))],
        out_specs=pl.BlockSpec((bq, H * LANES), lambda s, qa, ka, qb, kb, qc, kc, fl: (qc[s], 0)),
        scratch_shapes=[pltpu.VMEM((H, bq, bk), F32),
                        pltpu.VMEM((H, bq, bk), BF16),
                        pltpu.VMEM((H, bq, LANES), F32),
                        pltpu.VMEM((H, bq, LANES), F32),
                        pltpu.VMEM((H, bq, LANES), F32),
                        pltpu.VMEM((H, bq, LANES), F32),
                        pltpu.VMEM((H, bq, LANES), F32),
                        pltpu.VMEM((bq, bk), F32)],
    )
    return pl.pallas_call(
        functools.partial(_flash_kernel, mq_rows=mq_rows),
        grid_spec=grid_spec,
        out_shape=jax.ShapeDtypeStruct((T, H * LANES), BF16),
        compiler_params=_cparams("arbitrary"),
        name="dsa_flash",
    )(*tabs, qh, kh, vh, mask)


def _hgrn_static(C):
    levels = []
    m = C // 2
    while m >= 1:
        levels.append(m)
        m //= 2
    r = np.arange(C)[:, None]
    u = np.arange(C)[None, :]
    mats = [u <= r, u > r]
    masks = []
    for m in levels:
        mid = (r // (2 * m)) * 2 * m + m - 1
        lower = (r % (2 * m)) >= m
        mats.append(np.where(lower, (u > mid) & (u <= r), (u > r) & (u <= mid)))
        masks.append(((r // (2 * m)) == (u // (2 * m))) & lower & ((u % (2 * m)) < m))
    W = np.concatenate(mats, 0).astype(np.float32)
    return np.concatenate([W, W], 1), np.stack(masks).astype(np.float32)


def _hgrn_kernel(q_ref, f_ref, i_ref, og_ref, lb_ref, gw_ref, w2_ref, msk_ref, o_ref, st_ref, e_ref,
                 *, C, hp, layer):
    L = msk_ref.shape[0]

    @pl.when(pl.program_id(1) == 0)
    def _():
        st_ref[...] = jnp.zeros(st_ref.shape, F32)

    depth = lb_ref.shape[0]
    rows = [lb_ref[d:d + 1, :] for d in range(depth)]
    mx = functools.reduce(jnp.maximum, rows)
    ex = [jnp.exp(rw - mx) for rw in rows]
    den = functools.reduce(lambda a, b: a + b, ex)
    lb = jnp.zeros_like(mx)
    for d in range(1, layer + 1):
        lb = lb + ex[d] / den

    forget = lb + (1.0 - lb) * _sigmoid(f_ref[...])
    kin = 1.0 - forget
    g = jnp.log(forget)
    g_hi = g.astype(BF16)
    g_lo = (g - g_hi.astype(F32)).astype(BF16)
    g2 = jnp.concatenate([g_hi, g_lo], axis=0)
    e_ref[...] = jnp.dot(w2_ref[...], g2, preferred_element_type=F32)

    q = q_ref[...]
    qs = q * _sigmoid(q)
    og = og_ref[...]
    gate = og * _sigmoid(og)
    eye = lax.broadcasted_iota(I32, (C, C), 0) == lax.broadcasted_iota(I32, (C, C), 1)

    for hh in range(hp):
        sl = slice(hh * LANES, (hh + 1) * LANES)
        qh, kh, vh = qs[:, sl], kin[:, sl], i_ref[:, sl]
        st = st_ref[hh]
        o = lax.dot_general((qh * jnp.exp(e_ref[0:C, sl])).astype(BF16), st.astype(BF16), NT_DIMS,
                            preferred_element_type=F32)
        a = jnp.where(eye, jnp.sum(qh * kh, axis=1, keepdims=True), 0.0)
        for l in range(L):
            el = jnp.exp(e_ref[(2 + l) * C:(3 + l) * C, sl])
            a = a + msk_ref[l] * lax.dot_general((qh * el).astype(BF16), (kh * el).astype(BF16), NT_DIMS,
                                                 preferred_element_type=F32)
        vb = vh.astype(BF16)
        o = o + jnp.dot(a.astype(BF16), vb, preferred_element_type=F32)
        kd = (kh * jnp.exp(e_ref[C:2 * C, sl])).astype(BF16)
        st_ref[hh] = st * jnp.exp(e_ref[C - 1:C, sl]) + jnp.dot(vh.T.astype(BF16), kd, preferred_element_type=F32)
        o = o * lax.rsqrt(jnp.mean(o * o, axis=1, keepdims=True) + NORM_EPS) * gw_ref[...]
        o_ref[:, sl] = (o * gate[:, sl]).astype(o_ref.dtype)


def _hgrn(proj, lb_logits, gnorm_w, *, H, layer, C=128, hp=12):
    T = proj.shape[0]
    hp = max(d for d in range(1, hp + 1) if H % d == 0)
    wb = hp * LANES
    nb = H // hp
    w2, masks = _hgrn_static(C)
    L = masks.shape[0]
    col = lambda k: pl.BlockSpec((C, wb), lambda hb, c: (c, hb + k * nb))
    return pl.pallas_call(
        functools.partial(_hgrn_kernel, C=C, hp=hp, layer=layer),
        grid=(nb, T // C),
        in_specs=[col(0), col(1), col(2), col(3),
                  pl.BlockSpec((lb_logits.shape[0], wb), lambda hb, c: (0, hb)),
                  pl.BlockSpec((1, LANES), lambda hb, c: (0, 0)),
                  pl.BlockSpec(w2.shape, lambda hb, c: (0, 0)),
                  pl.BlockSpec(masks.shape, lambda hb, c: (0, 0, 0))],
        out_specs=pl.BlockSpec((C, wb), lambda hb, c: (c, hb)),
        out_shape=jax.ShapeDtypeStruct((T, H * LANES), BF16),
        scratch_shapes=[pltpu.VMEM((hp, LANES, LANES), F32),
                        pltpu.VMEM(((2 + L) * C, wb), F32)],
        compiler_params=_cparams("parallel", "arbitrary"),
        name="hgrn2",
    )(proj, proj, proj, proj, lb_logits, gnorm_w.reshape(1, LANES), jnp.asarray(w2, BF16), jnp.asarray(masks))


def _outproj_kernel(mix_ref, mq_ref, mk_ref, mv_ref, w_ref, x_ref, o_ref, cat_ref):
    Wm = mix_ref.shape[1]
    scale = HEAD_DIM ** -0.5
    cat_ref[:, :Wm] = mix_ref[...]
    for h in range(mq_ref.shape[1] // LANES):
        sl = slice(h * LANES, (h + 1) * LANES)
        logits = lax.dot_general(mq_ref[:, sl].astype(BF16), mk_ref[:, sl], NT_DIMS,
                                 preferred_element_type=F32) * scale
        p = jnp.exp(logits - jnp.max(logits, axis=1, keepdims=True))
        den = jnp.sum(p, axis=1, keepdims=True)
        o = jnp.dot(p.astype(BF16), mv_ref[:, sl], preferred_element_type=F32) / den
        cat_ref[:, Wm + h * LANES:Wm + (h + 1) * LANES] = o.astype(BF16)
    o_ref[...] = x_ref[...] + jnp.dot(cat_ref[...], w_ref[...], preferred_element_type=F32)


def _outproj(mix, proj, mq_block, mem_kv, w_out, x):
    T, D = x.shape
    Wm = mix.shape[1]
    Wmem = D - Wm
    M = mem_kv.shape[0]
    tm = _pick_tile(T, 512, 16)
    return pl.pallas_call(
        _outproj_kernel,
        grid=(T // tm,),
        in_specs=[pl.BlockSpec((tm, Wm), lambda i: (i, 0)),
                  pl.BlockSpec((tm, Wmem), lambda i: (i, mq_block)),
                  pl.BlockSpec((M, Wmem), lambda i: (0, 0)),
                  pl.BlockSpec((M, Wmem), lambda i: (0, 1)),
                  pl.BlockSpec((D, D), lambda i: (0, 0)),
                  pl.BlockSpec((tm, D), lambda i: (i, 0))],
        out_specs=pl.BlockSpec((tm, D), lambda i: (i, 0)),
        out_shape=jax.ShapeDtypeStruct((T, D), F32),
        scratch_shapes=[pltpu.VMEM((tm, D), BF16)],
        compiler_params=_cparams("parallel"),
        name="outproj",
    )(mix, proj, mem_kv, mem_kv, w_out, x)


def _ffn_kernel(x_ref, xh_ref, g_ref, wg_ref, wu_ref, cw_ref, cb_ref, wd_ref, fw_ref, o_ref,
                h_ref, hh_ref, acc_ref, *, final_norm):
    i, f = pl.program_id(0), pl.program_id(1)
    tm, tf = acc_ref.shape[0], wg_ref.shape[1]

    @pl.when(f == 0)
    def _():
        _rms_rows(x_ref, g_ref, h_ref)
        _rms_rows(xh_ref, g_ref, hh_ref)
        hh_ref[...] = jnp.where(i > 0, hh_ref[...], jnp.zeros_like(hh_ref))
        acc_ref[...] = jnp.zeros(acc_ref.shape, F32)

    h = h_ref[...]
    gate = jnp.dot(h, wg_ref[...], preferred_element_type=F32)
    halo = jnp.dot(hh_ref[...], wg_ref[...], preferred_element_type=F32)
    rows = hh_ref.shape[0]
    prev1 = jnp.broadcast_to(halo[rows - 1:rows, :], (tm, tf))
    prev2 = jnp.broadcast_to(halo[rows - 2:rows - 1, :], (tm, tf))
    row = lax.broadcasted_iota(I32, (tm, tf), 0)
    g1 = jnp.where(row < 1, prev1, pltpu.roll(gate, 1, 0))
    g2 = jnp.where(row < 1, prev2, jnp.where(row < 2, prev1, pltpu.roll(gate, 2, 0)))
    cw = cw_ref[...]
    gt = cw[0:1, :] * g2 + cw[1:2, :] * g1 + cw[2:3, :] * gate + cb_ref[...]
    up = jnp.dot(h, wu_ref[...], preferred_element_type=F32)
    act = (gt * _sigmoid(gt) * up).astype(BF16)
    acc_ref[...] += jnp.dot(act, wd_ref[...], preferred_element_type=F32)

    @pl.when(f == pl.num_programs(1) - 1)
    def _():
        if final_norm:
            acc_ref[...] += x_ref[...]
            _rms_rows(acc_ref, fw_ref, o_ref)
        else:
            o_ref[...] = x_ref[...] + acc_ref[...]


def _ffn(x, ln_w, w_gate, w_up, conv_w, conv_b, w_down, final_w, *, final_norm):
    T, D = x.shape
    F = w_gate.shape[1]
    tm = _pick_tile(T, 512, 16)
    tf = _pick_tile(F, 512)
    halo = 8
    vec = lambda n: pl.BlockSpec((1, n), lambda i, f: (0, 0))
    return pl.pallas_call(
        functools.partial(_ffn_kernel, final_norm=final_norm),
        grid=(T // tm, F // tf),
        in_specs=[pl.BlockSpec((tm, D), lambda i, f: (i, 0)),
                  pl.BlockSpec((halo, D), lambda i, f: (jnp.maximum(i * (tm // halo) - 1, 0), 0)),
                  vec(D),
                  pl.BlockSpec((D, tf), lambda i, f: (0, f)),
                  pl.BlockSpec((D, tf), lambda i, f: (0, f)),
                  pl.BlockSpec((CONV_WIDTH, tf), lambda i, f: (0, f)),
                  pl.BlockSpec((1, tf), lambda i, f: (0, f)),
                  pl.BlockSpec((tf, D), lambda i, f: (f, 0)),
                  vec(D)],
        out_specs=pl.BlockSpec((tm, D), lambda i, f: (i, 0)),
        out_shape=jax.ShapeDtypeStruct((T, D), F32),
        scratch_shapes=[pltpu.VMEM((tm, D), BF16), pltpu.VMEM((halo, D), BF16), pltpu.VMEM((tm, D), F32)],
        compiler_params=_cparams("parallel", "arbitrary"),
        name="conv_ffn",
    )(x, x, ln_w.reshape(1, D), w_gate, w_up, conv_w, conv_b.reshape(1, F), w_down, final_w.reshape(1, D))


def _dsa_weight_layout(w, Wm, Wmem):
    n_iq = IDX_HEADS * IDX_DIM
    o = 3 * Wm + n_iq
    ik_iw = w[:, o:o + IDX_DIM + IDX_HEADS]
    mq = w[:, o + IDX_DIM + IDX_HEADS:]
    pad = jnp.zeros((w.shape[0], LANES - IDX_DIM - IDX_HEADS), w.dtype)
    return jnp.concatenate([w[:, :o], mq, ik_iw, pad], axis=1)


def _trunk(x, mem, pos, ln1_w, ln2_w, dsa_w_in, hgrn_w_in, hgrn_lb_logits, hgrn_gnorm_w, mem_norm_w,
           w_mem_kv, w_out, ffn_w_gate, ffn_w_up, ffn_conv_w, ffn_conv_b, ffn_w_down, final_norm_w):
    T, D = x.shape
    Wmem = w_mem_kv.shape[1] // 2
    Wm = D - Wmem
    H = Wm // HEAD_DIM
    depth = ln1_w.shape[0]
    n_sel = min(TOPK_MAX, T // 4)

    mem_kv = _norm_matmul(mem, mem_norm_w, w_mem_kv.astype(BF16), out_dtype=BF16, tn_cap=Wmem, name="mem_kv")
    cos_t, sin_t = _rope_tables(pos.reshape(T, 1))

    for layer in range(depth):
        j = layer // N_MIXERS
        if layer % N_MIXERS == 0:
            w_in = _dsa_weight_layout(dsa_w_in[j], Wm, Wmem).astype(BF16)
            proj = _norm_matmul(x, ln1_w[layer], w_in, out_dtype=F32, tn_cap=1024, name="dsa_proj")
            qh, kh, vh, iq, ika, ikb, iw = _dsa_prep(proj, cos_t, sin_t, H=H)
            bq_idx = min(256, T)
            kc = min(512, T)
            mask = _indexer(iq, ika, ikb, iw, bq=bq_idx, kc=kc, n_sel=n_sel)
            mix = _flash(qh, kh, vh, mask, bq=min(256, T), bk=kc)
            mq_block = (3 * Wm + IDX_HEADS * IDX_DIM) // Wmem
        else:
            proj = _norm_matmul(x, ln1_w[layer], hgrn_w_in[j].astype(BF16), out_dtype=F32, tn_cap=1024,
                                name="hgrn_proj")
            mix = _hgrn(proj, hgrn_lb_logits, hgrn_gnorm_w[j], H=H, layer=layer)
            mq_block = 4 * Wm // Wmem
        x = _outproj(mix, proj, mq_block, mem_kv, w_out[layer].astype(BF16), x)
        x = _ffn(x, ln2_w[layer], ffn_w_gate[layer].astype(BF16), ffn_w_up[layer].astype(BF16),
                 ffn_conv_w[layer], ffn_conv_b[layer], ffn_w_down[layer].astype(BF16), final_norm_w,
                 final_norm=(layer == depth - 1))
    return x


def kernel(x, mem, positions, ln1_w, ln2_w, dsa_w_in, hgrn_w_in, hgrn_lb_logits, hgrn_gnorm_w, mem_norm_w,
           w_mem_kv, w_out, ffn_w_gate, ffn_w_up, ffn_conv_w, ffn_conv_b, ffn_w_down, final_norm_w):
    outs = [_trunk(x[b], mem[b], positions[b], ln1_w, ln2_w, dsa_w_in, hgrn_w_in, hgrn_lb_logits,
                   hgrn_gnorm_w, mem_norm_w, w_mem_kv, w_out, ffn_w_gate, ffn_w_up, ffn_conv_w,
                   ffn_conv_b, ffn_w_down, final_norm_w)
            for b in range(x.shape[0])]
    return jnp.stack(outs, axis=0)
```

```python
import functools

import numpy as np
import jax
import jax.numpy as jnp
from jax import lax
from jax.experimental import pallas as pl
from jax.experimental.pallas import tpu as pltpu

F32 = jnp.float32
BF16 = jnp.bfloat16
I32 = jnp.int32

LANES = 128
HEAD_DIM = 128
IDX_HEADS = 16
IDX_DIM = 64
TOPK_MAX = 256
ROPE_THETA = 500000.0
ROT_FRACTION = 4
N_MIXERS = 2
CONV_WIDTH = 3
NORM_EPS = 1e-6
NEG_INF = -1e30
INT_MIN = -(2 ** 31)

V7X_VMEM_LIMIT_BYTES = 56 * 1024 * 1024

NT_DIMS = (((1,), (1,)), ((), ()))
Q_SCALE_LOG2 = float(HEAD_DIM ** -0.5 * np.log2(np.e))
FLAG_B_FIRST, FLAG_C_FIRST, FLAG_B_LAST, FLAG_C_LAST = 1, 2, 4, 8
UNTESTED_SEARCH_PASSES = 12


def _cparams(*sem):
    return pltpu.CompilerParams(dimension_semantics=sem,
                                vmem_limit_bytes=V7X_VMEM_LIMIT_BYTES)


def _pick_tile(n, cap, mult=LANES):
    best = None
    t = mult
    while t <= min(n, cap):
        if n % t == 0:
            best = t
        t += mult
    assert best is not None, (n, cap, mult)
    return best


def _sigmoid(x):
    return 1.0 / (1.0 + jnp.exp(-x))


def _rms_rows(x_ref, g_ref, out_ref, chunk=64):
    rows = x_ref.shape[0]
    chunk = min(chunk, rows)

    def body(c, carry):
        r = pl.multiple_of(c * chunk, chunk)
        x = x_ref[pl.ds(r, chunk), :].astype(F32)
        ms = jnp.mean(x * x, axis=-1, keepdims=True)
        out_ref[pl.ds(r, chunk), :] = (x * lax.rsqrt(ms + NORM_EPS) * g_ref[...]).astype(out_ref.dtype)
        return carry

    lax.fori_loop(0, rows // chunk, body, 0)


def _rope_lane_constants():
    def pattern(period, rot):
        half = rot // 2
        inv = (np.float32(ROPE_THETA) ** (-(np.arange(half, dtype=np.float32) / np.float32(half)))).astype(np.float32)
        lane = np.arange(LANES) % period
        freq = np.where(lane < rot, inv[lane % half], 0.0).astype(np.float32)
        sign = np.where(lane < half, -1.0, np.where(lane < rot, 1.0, 0.0)).astype(np.float32)
        return freq, sign
    f_att, s_att = pattern(HEAD_DIM, HEAD_DIM // ROT_FRACTION)
    f_idx, s_idx = pattern(IDX_DIM, IDX_DIM // ROT_FRACTION)
    return (np.concatenate([f_att, f_idx])[None, :], np.concatenate([s_att, s_idx])[None, :])


def _rope_tables_kernel(pos_ref, f_ref, s_ref, cos_ref, sin_ref):
    ang = pos_ref[...].astype(F32) * f_ref[...]
    cos_ref[...] = jnp.cos(ang)
    sin_ref[...] = jnp.sin(ang) * s_ref[...]


def _rope_tables(pos_col):
    T = pos_col.shape[0]
    tm = _pick_tile(T, 512, 8)
    freq, sign = _rope_lane_constants()
    row = pl.BlockSpec((tm, 2 * LANES), lambda i: (i, 0))
    const = pl.BlockSpec((1, 2 * LANES), lambda i: (0, 0))
    return pl.pallas_call(
        _rope_tables_kernel,
        grid=(T // tm,),
        in_specs=[pl.BlockSpec((tm, 1), lambda i: (i, 0)), const, const],
        out_specs=[row, row],
        out_shape=[jax.ShapeDtypeStruct((T, 2 * LANES), F32)] * 2,
        compiler_params=_cparams("parallel"),
        name="rope_tables",
    )(pos_col, jnp.asarray(freq), jnp.asarray(sign))


def _norm_matmul_kernel(x_ref, g_ref, w_ref, o_ref, h_ref):
    @pl.when(pl.program_id(1) == 0)
    def _():
        _rms_rows(x_ref, g_ref, h_ref)

    o_ref[...] = jnp.dot(h_ref[...], w_ref[...], preferred_element_type=F32).astype(o_ref.dtype)


def _norm_matmul(x, g, w, *, out_dtype, tm_cap=1024, tn_cap=1024, name):
    T, D = x.shape
    N = w.shape[1]
    tm = _pick_tile(T, tm_cap, 8)
    tn = _pick_tile(N, tn_cap)
    return pl.pallas_call(
        _norm_matmul_kernel,
        grid=(T // tm, N // tn),
        in_specs=[pl.BlockSpec((tm, D), lambda i, j: (i, 0)),
                  pl.BlockSpec((1, D), lambda i, j: (0, 0)),
                  pl.BlockSpec((D, tn), lambda i, j: (0, j))],
        out_specs=pl.BlockSpec((tm, tn), lambda i, j: (i, j)),
        out_shape=jax.ShapeDtypeStruct((T, N), out_dtype),
        scratch_shapes=[pltpu.VMEM((tm, D), BF16)],
        compiler_params=_cparams("parallel", "arbitrary"),
        name=name,
    )(x, g.reshape(1, D), w)


def _rope_apply(x, cos, sin, half, period):
    lane = lax.broadcasted_iota(I32, x.shape, 1)
    up = pltpu.roll(x, LANES - half, 1)
    dn = pltpu.roll(x, half, 1)
    partner = jnp.where((lane % period) < half, up, dn)
    return x * cos + partner * sin


def _dsa_prep_kernel(p_ref, cos_ref, sin_ref, q_ref, k_ref, v_ref, iq_ref, ika_ref, ikb_ref, iw_ref, *, H):
    Wm = H * HEAD_DIM
    G = IDX_HEADS * IDX_DIM // LANES
    cos_a, sin_a = cos_ref[:, :LANES], sin_ref[:, :LANES]
    cos_i, sin_i = cos_ref[:, LANES:], sin_ref[:, LANES:]
    half_a = HEAD_DIM // ROT_FRACTION // 2
    half_i = IDX_DIM // ROT_FRACTION // 2
    for h in range(H):
        q = p_ref[:, h * LANES:(h + 1) * LANES]
        q_ref[h] = (_rope_apply(q, cos_a, sin_a, half_a, HEAD_DIM) * Q_SCALE_LOG2).astype(BF16)
        k = p_ref[:, Wm + h * LANES:Wm + (h + 1) * LANES]
        k_ref[h] = _rope_apply(k, cos_a, sin_a, half_a, HEAD_DIM).astype(BF16)
        v_ref[h] = p_ref[:, 2 * Wm + h * LANES:2 * Wm + (h + 1) * LANES].astype(BF16)
    for g in range(G):
        x = p_ref[:, 3 * Wm + g * LANES:3 * Wm + (g + 1) * LANES]
        iq_ref[g] = _rope_apply(x, cos_i, sin_i, half_i, IDX_DIM).astype(BF16)
    tail = p_ref[:, p_ref.shape[1] - LANES:]
    lane = lax.broadcasted_iota(I32, tail.shape, 1)
    ik = jnp.where(lane < IDX_DIM, _rope_apply(tail, cos_i, sin_i, half_i, IDX_DIM), 0.0)
    ika_ref[...] = ik.astype(BF16)
    ikb_ref[...] = pltpu.roll(ik, IDX_DIM, 1).astype(BF16)
    iw = pltpu.roll(tail, LANES - IDX_DIM, 1)
    iw_ref[...] = jnp.where(lane < IDX_HEADS, iw * (IDX_HEADS ** -0.5 * IDX_DIM ** -0.5), 0.0)


def _dsa_prep(proj, cos_t, sin_t, *, H):
    T, N = proj.shape
    G = IDX_HEADS * IDX_DIM // LANES
    tm = _pick_tile(T, 256, 16)
    hm = lambda n: pl.BlockSpec((n, tm, LANES), lambda i: (0, i, 0))
    row = pl.BlockSpec((tm, LANES), lambda i: (i, 0))
    tab = pl.BlockSpec((tm, 2 * LANES), lambda i: (i, 0))
    return pl.pallas_call(
        functools.partial(_dsa_prep_kernel, H=H),
        grid=(T // tm,),
        in_specs=[pl.BlockSpec((tm, N), lambda i: (i, 0)), tab, tab],
        out_specs=[hm(H), hm(H), hm(H), hm(G), row, row, row],
        out_shape=[jax.ShapeDtypeStruct((H, T, LANES), BF16)] * 3
        + [jax.ShapeDtypeStruct((G, T, LANES), BF16)]
        + [jax.ShapeDtypeStruct((T, LANES), BF16)] * 2
        + [jax.ShapeDtypeStruct((T, LANES), F32)],
        compiler_params=_cparams("parallel"),
        name="dsa_prep",
    )(proj, cos_t, sin_t)


def _indexer_kernel(iq_ref, ika_ref, ikb_ref, iw_ref, mask_ref, key_ref, iwb_ref, s_ref, tie_ref,
                    *, bq, kc, n_sel):
    i = pl.program_id(0)
    G = iq_ref.shape[0]
    nkc = key_ref.shape[0]
    reps = kc // LANES
    idx_bits = (nkc * kc - 1).bit_length()
    n_c = ((i + 1) * bq + kc - 1) // kc

    iw = iw_ref[...]
    for h in range(IDX_HEADS):
        iwb_ref[h] = jnp.broadcast_to(iw[:, h:h + 1], (bq, LANES))

    iq = iq_ref[...].reshape(G * bq, LANES)
    qpos = i * bq + lax.broadcasted_iota(I32, (bq, kc), 0)
    lane_k = lax.broadcasted_iota(I32, (bq, kc), 1)

    def score_chunk(c, carry):
        k0 = pl.multiple_of(c * kc, kc)
        score = jnp.zeros((bq, kc), F32)
        for half, ik_ref in enumerate((ika_ref, ikb_ref)):
            s_ref[...] = lax.dot_general(iq, ik_ref[pl.ds(k0, kc), :], NT_DIMS, preferred_element_type=F32)
            for g in range(G):
                w = pltpu.repeat(iwb_ref[2 * g + half], reps, 1)
                score = score + jnp.maximum(s_ref[g * bq:(g + 1) * bq, :], 0.0) * w
        score = jnp.where(k0 + lane_k <= qpos, score, NEG_INF)
        key_ref[c] = sortable(score)
        top1, top2 = carry
        for l in range(reps):
            x = score[:, l * LANES:(l + 1) * LANES]
            top2 = jnp.maximum(top2, jnp.minimum(top1, x))
            top1 = jnp.maximum(top1, x)
        return top1, top2

    def sortable(x):
        bits = pltpu.bitcast(x, I32)
        return bits ^ ((bits >> 31) & 0x7FFFFFFF)

    row_shape = (bq, LANES)
    lowest = jnp.full(row_shape, jnp.finfo(F32).min, F32)
    top1, top2 = lax.fori_loop(0, n_c, score_chunk, (lowest, lowest))

    def row_sum(acc):
        return jnp.broadcast_to(jnp.sum(acc, axis=1, keepdims=True), row_shape)

    def count_ge(cmp):
        def body(c, acc):
            k = key_ref[c]
            for l in range(reps):
                acc = acc + jnp.where(k[:, l * LANES:(l + 1) * LANES] >= cmp, 1.0, 0.0)
            return acc

        return row_sum(lax.fori_loop(0, n_c, body, jnp.zeros(row_shape, F32)))

    lo = sortable(jnp.broadcast_to(jnp.min(top2, axis=1, keepdims=True), row_shape)) ^ INT_MIN
    hi = (sortable(jnp.broadcast_to(jnp.max(top1, axis=1, keepdims=True), row_shape)) ^ INT_MIN) + 1

    def bisect(lo, hi, cnt_lo):
        mid = lax.shift_right_logical(lo, 1) + lax.shift_right_logical(hi, 1) + (lo & hi & 1)
        cnt = count_ge(mid ^ INT_MIN)
        take = cnt >= n_sel
        return jnp.where(take, mid, lo), jnp.where(take, hi, mid), jnp.where(take, cnt, cnt_lo)

    def search_step(c):
        lo, hi, cnt_lo = bisect(*c[:3])
        unsettled = jnp.where(cnt_lo == n_sel, 0.0, jnp.where(hi - lo == 1, 0.0, 1.0))
        return lo, hi, cnt_lo, jnp.max(unsettled)

    lo, hi, cnt_t = lax.fori_loop(0, UNTESTED_SEARCH_PASSES, lambda _, c: bisect(*c),
                                  (lo, hi, count_ge(lo ^ INT_MIN)))
    lo, _, cnt_t, _ = lax.while_loop(lambda c: c[3] > 0.0, search_step, (lo, hi, cnt_t, jnp.float32(1.0)))
    thr_row = lo ^ INT_MIN
    thr = pltpu.repeat(thr_row, reps, 1)

    tie_ref[...] = jnp.full(row_shape, nkc * kc, I32)
    excess = jnp.where(cnt_t > n_sel, 1.0, 0.0)

    @pl.when(jnp.max(excess) > 0.0)
    def _():
        need = n_sel - count_ge(thr_row + 1)
        lane = lax.broadcasted_iota(I32, row_shape, 1)

        def count_tied_below(u):
            def body(c, acc):
                k = key_ref[c]
                for l in range(reps):
                    idx = c * kc + l * LANES + lane
                    tied = k[:, l * LANES:(l + 1) * LANES] == thr_row
                    acc = acc + jnp.where(tied, jnp.where(idx < u, 1.0, 0.0), 0.0)
                return acc

            return row_sum(lax.fori_loop(0, n_c, body, jnp.zeros(row_shape, F32)))

        def idx_step(b, u):
            cand = u | jnp.left_shift(jnp.int32(1), idx_bits - 1 - b)
            return jnp.where(count_tied_below(cand) < need, cand, u)

        u = lax.fori_loop(0, idx_bits, idx_step, jnp.zeros(row_shape, I32))
        tie_ref[...] = jnp.where(excess > 0.0, u, nkc * kc)

    tie = pltpu.repeat(tie_ref[...], reps, 1)

    def emit(c, carry):
        k0 = pl.multiple_of(c * kc, kc)
        kidx = k0 + lane_k
        k = key_ref[c]
        sel = jnp.where(k > thr, 1, jnp.where(k == thr, jnp.where(kidx <= tie, 1, 0), 0))
        mask_ref[0, c] = jnp.where(kidx <= qpos, sel, 0).astype(jnp.int8)
        return carry

    lax.fori_loop(0, n_c, emit, 0)

    def clear(c, carry):
        mask_ref[0, c] = jnp.zeros((bq, kc), jnp.int8)
        return carry

    lax.fori_loop(n_c, nkc, clear, 0)


def _indexer(iq, ika, ikb, iw, *, bq, kc, n_sel):
    G, T, _ = iq.shape
    assert n_sel <= 2 * LANES, "the search bracket counts two keys per lane"
    nq, nkc = T // bq, T // kc
    full = pl.BlockSpec((T, LANES), lambda i: (0, 0))
    row = pl.BlockSpec((bq, LANES), lambda i: (i, 0))
    return pl.pallas_call(
        functools.partial(_indexer_kernel, bq=bq, kc=kc, n_sel=n_sel),
        grid=(nq,),
        in_specs=[pl.BlockSpec((G, bq, LANES), lambda i: (0, i, 0)), full, full, row],
        out_specs=pl.BlockSpec((1, nkc, bq, kc), lambda i: (i, 0, 0, 0)),
        out_shape=jax.ShapeDtypeStruct((nq, nkc, bq, kc), jnp.int8),
        scratch_shapes=[pltpu.VMEM((nkc, bq, kc), I32),
                        pltpu.VMEM((IDX_HEADS, bq, LANES), F32),
                        pltpu.VMEM((G * bq, kc), F32),
                        pltpu.VMEM((bq, LANES), I32)],
        compiler_params=_cparams("parallel"),
        name="dsa_indexer",
    )(iq, ika, ikb, iw)


def _flash_kernel(qa_tab, ka_tab, qb_tab, kb_tab, qc_tab, kc_tab, flag_tab,
                  q_ref, k_ref, v_ref, mask_ref, o_ref,
                  logit_ref, p_ref, alpha_ref, m_ref, l_ref, linv_ref, acc_ref, bias_ref, *, mq_rows):
    s = pl.program_id(0)
    H, bq = q_ref.shape[0], q_ref.shape[1]
    bk = k_ref.shape[1]
    reps = bk // LANES
    flags = flag_tab[s]
    b_first = (flags & FLAG_B_FIRST) != 0
    c_first = (flags & FLAG_C_FIRST) != 0

    @pl.when(s == 0)
    def _():
        logit_ref[...] = jnp.zeros(logit_ref.shape, F32)
        p_ref[...] = jnp.zeros(p_ref.shape, BF16)
        alpha_ref[...] = jnp.zeros(alpha_ref.shape, F32)
        m_ref[...] = jnp.zeros(m_ref.shape, F32)
        l_ref[...] = jnp.zeros(l_ref.shape, F32)
        linv_ref[...] = jnp.zeros(linv_ref.shape, F32)
        acc_ref[...] = jnp.zeros(acc_ref.shape, F32)

    for r in range(bq // mq_rows):
        sel = mask_ref[r, 0].astype(F32)
        bias_ref[r * mq_rows:(r + 1) * mq_rows, :] = (sel - 1.0) * (-NEG_INF)

    for h in range(H):
        acc_prev = jnp.where(c_first, 0.0, acc_ref[h])
        acc_ref[h] = acc_prev * alpha_ref[h] + jnp.dot(p_ref[h], v_ref[h], preferred_element_type=F32)
        m_prev = jnp.where(b_first, NEG_INF, m_ref[h])
        m_next = jnp.maximum(m_prev, jnp.max(logit_ref[h], axis=1, keepdims=True))
        m_ref[h] = m_next
        alpha_ref[h] = jnp.exp2(m_prev - m_next)

    for h in range(H):
        p = jnp.exp2(logit_ref[h] - pltpu.repeat(m_ref[h], reps, 1))
        l_prev = jnp.where(b_first, 0.0, l_ref[h])
        l_ref[h] = alpha_ref[h] * l_prev + jnp.sum(p, axis=1, keepdims=True)
        p_ref[h] = p.astype(BF16)
        logit_ref[h] = lax.dot_general(q_ref[h], k_ref[h], NT_DIMS, preferred_element_type=F32) + bias_ref[...]

    @pl.when((flags & FLAG_C_LAST) != 0)
    def _():
        for h in range(H):
            o_ref[:, h * LANES:(h + 1) * LANES] = (acc_ref[h] * linv_ref[h]).astype(o_ref.dtype)

    @pl.when((flags & FLAG_B_LAST) != 0)
    def _():
        for h in range(H):
            linv_ref[h] = 1.0 / l_ref[h]


def _flash(qh, kh, vh, mask, *, bq, bk):
    H, T, _ = qh.shape
    mq_rows = mask.shape[2]
    assert mask.shape[3] == bk and bq % mq_rows == 0
    tiles = [(qi, ki) for qi in range(T // bq) for ki in range(((qi + 1) * bq - 1) // bk + 1)]
    n = len(tiles)
    n_steps = n + 2

    def lagged(lag):
        idx = np.clip(np.arange(n_steps) - lag, 0, n - 1)
        return (np.array([tiles[i][0] for i in idx], np.int32), np.array([tiles[i][1] for i in idx], np.int32))

    (qa, ka), (qb, kb), (qc, kc) = lagged(0), lagged(1), lagged(2)
    step = np.arange(n_steps)
    b_live, c_live = (step >= 1) & (step <= n), step >= 2
    last = lambda q: ((q + 1) * bq - 1) // bk
    flags = (np.where(kb == 0, FLAG_B_FIRST, 0) | np.where(kc == 0, FLAG_C_FIRST, 0)
             | np.where(b_live & (kb == last(qb)), FLAG_B_LAST, 0)
             | np.where(c_live & (kc == last(qc)), FLAG_C_LAST, 0)).astype(np.int32)
    tabs = [jnp.asarray(t) for t in (qa, ka, qb, kb, qc, kc, flags)]
    grid_spec = pltpu.PrefetchScalarGridSpec(
        num_scalar_prefetch=len(tabs),
        grid=(n_steps,),
        in_specs=[pl.BlockSpec((H, bq, LANES), lambda s, qa, ka, qb, kb, qc, kc, fl: (0, qa[s], 0)),
                  pl.BlockSpec((H, bk, LANES), lambda s, qa, ka, qb, kb, qc, kc, fl: (0, ka[s], 0)),
                  pl.BlockSpec((H, bk, LANES), lambda s, qa, ka, qb, kb, qc, kc, fl: (0, kc[s], 0)),
                  pl.BlockSpec((bq // mq_rows, 1, mq_rows, bk),
                               lambda s, qa, ka, qb, kb, qc, kc, fl: (qa[s], ka[s], 0, 0))],
        out_specs=pl.BlockSpec((bq, H * LANES), lambda s, qa, ka, qb, kb, qc, kc, fl: (qc[s], 0)),
        scratch_shapes=[pltpu.VMEM((H, bq, bk), F32),
                        pltpu.VMEM((H, bq, bk), BF16),
                        pltpu.VMEM((H, bq, LANES), F32),
                        pltpu.VMEM((H, bq, LANES), F32),
                        pltpu.VMEM((H, bq, LANES), F32),
                        pltpu.VMEM((H, bq, LANES), F32),
                        pltpu.VMEM((H, bq, LANES), F32),
                        pltpu.VMEM((bq, bk), F32)],
    )
    return pl.pallas_call(
        functools.partial(_flash_kernel, mq_rows=mq_rows),
        grid_spec=grid_spec,
        out_shape=jax.ShapeDtypeStruct((T, H * LANES), BF16),
        compiler_params=_cparams("arbitrary"),
        name="dsa_flash",
    )(*tabs, qh, kh, vh, mask)


def _hgrn_static(C):
    levels = []
    m = C // 2
    while m >= 1:
        levels.append(m)
        m //= 2
    r = np.arange(C)[:, None]
    u = np.arange(C)[None, :]
    mats = [u <= r, u > r]
    masks = []
    for m in levels:
        mid = (r // (2 * m)) * 2 * m + m - 1
        lower = (r % (2 * m)) >= m
        mats.append(np.where(lower, (u > mid) & (u <= r), (u > r) & (u <= mid)))
        masks.append(((r // (2 * m)) == (u // (2 * m))) & lower & ((u % (2 * m)) < m))
    W = np.concatenate(mats, 0).astype(np.float32)
    return np.concatenate([W, W], 1), np.stack(masks).astype(np.float32)


def _hgrn_kernel(q_ref, f_ref, i_ref, og_ref, lb_ref, gw_ref, w2_ref, msk_ref, o_ref, st_ref, e_ref,
                 *, C, hp, layer):
    L = msk_ref.shape[0]

    @pl.when(pl.program_id(1) == 0)
    def _():
        st_ref[...] = jnp.zeros(st_ref.shape, F32)

    depth = lb_ref.shape[0]
    rows = [lb_ref[d:d + 1, :] for d in range(depth)]
    mx = functools.reduce(jnp.maximum, rows)
    ex = [jnp.exp(rw - mx) for rw in rows]
    den = functools.reduce(lambda a, b: a + b, ex)
    lb = jnp.zeros_like(mx)
    for d in range(1, layer + 1):
        lb = lb + ex[d] / den

    forget = lb + (1.0 - lb) * _sigmoid(f_ref[...])
    kin = 1.0 - forget
    g = jnp.log(forget)
    g_hi = g.astype(BF16)
    g_lo = (g - g_hi.astype(F32)).astype(BF16)
    g2 = jnp.concatenate([g_hi, g_lo], axis=0)
    e_ref[...] = jnp.dot(w2_ref[...], g2, preferred_element_type=F32)

    q = q_ref[...]
    qs = q * _sigmoid(q)
    og = og_ref[...]
    gate = og * _sigmoid(og)
    eye = lax.broadcasted_iota(I32, (C, C), 0) == lax.broadcasted_iota(I32, (C, C), 1)

    for hh in range(hp):
        sl = slice(hh * LANES, (hh + 1) * LANES)
        qh, kh, vh = qs[:, sl], kin[:, sl], i_ref[:, sl]
        st = st_ref[hh]
        o = lax.dot_general((qh * jnp.exp(e_ref[0:C, sl])).astype(BF16), st.astype(BF16), NT_DIMS,
                            preferred_element_type=F32)
        a = jnp.where(eye, jnp.sum(qh * kh, axis=1, keepdims=True), 0.0)
        for l in range(L):
            el = jnp.exp(e_ref[(2 + l) * C:(3 + l) * C, sl])
            a = a + msk_ref[l] * lax.dot_general((qh * el).astype(BF16), (kh * el).astype(BF16), NT_DIMS,
                                                 preferred_element_type=F32)
        vb = vh.astype(BF16)
        o = o + jnp.dot(a.astype(BF16), vb, preferred_element_type=F32)
        kd = (kh * jnp.exp(e_ref[C:2 * C, sl])).astype(BF16)
        st_ref[hh] = st * jnp.exp(e_ref[C - 1:C, sl]) + jnp.dot(vh.T.astype(BF16), kd, preferred_element_type=F32)
        o = o * lax.rsqrt(jnp.mean(o * o, axis=1, keepdims=True) + NORM_EPS) * gw_ref[...]
        o_ref[:, sl] = (o * gate[:, sl]).astype(o_ref.dtype)


def _hgrn(proj, lb_logits, gnorm_w, *, H, layer, C=128, hp=12):
    T = proj.shape[0]
    hp = max(d for d in range(1, hp + 1) if H % d == 0)
    wb = hp * LANES
    nb = H // hp
    w2, masks = _hgrn_static(C)
    L = masks.shape[0]
    col = lambda k: pl.BlockSpec((C, wb), lambda hb, c: (c, hb + k * nb))
    return pl.pallas_call(
        functools.partial(_hgrn_kernel, C=C, hp=hp, layer=layer),
        grid=(nb, T // C),
        in_specs=[col(0), col(1), col(2), col(3),
                  pl.BlockSpec((lb_logits.shape[0], wb), lambda hb, c: (0, hb)),
                  pl.BlockSpec((1, LANES), lambda hb, c: (0, 0)),
                  pl.BlockSpec(w2.shape, lambda hb, c: (0, 0)),
                  pl.BlockSpec(masks.shape, lambda hb, c: (0, 0, 0))],
        out_specs=pl.BlockSpec((C, wb), lambda hb, c: (c, hb)),
        out_shape=jax.ShapeDtypeStruct((T, H * LANES), BF16),
        scratch_shapes=[pltpu.VMEM((hp, LANES, LANES), F32),
                        pltpu.VMEM(((2 + L) * C, wb), F32)],
        compiler_params=_cparams("parallel", "arbitrary"),
        name="hgrn2",
    )(proj, proj, proj, proj, lb_logits, gnorm_w.reshape(1, LANES), jnp.asarray(w2, BF16), jnp.asarray(masks))


def _outproj_kernel(mix_ref, mq_ref, mk_ref, mv_ref, w_ref, x_ref, o_ref, cat_ref):
    Wm = mix_ref.shape[1]
    scale = HEAD_DIM ** -0.5
    cat_ref[:, :Wm] = mix_ref[...]
    for h in range(mq_ref.shape[1] // LANES):
        sl = slice(h * LANES, (h + 1) * LANES)
        logits = lax.dot_general(mq_ref[:, sl].astype(BF16), mk_ref[:, sl], NT_DIMS,
                                 preferred_element_type=F32) * scale
        p = jnp.exp(logits - jnp.max(logits, axis=1, keepdims=True))
        den = jnp.sum(p, axis=1, keepdims=True)
        o = jnp.dot(p.astype(BF16), mv_ref[:, sl], preferred_element_type=F32) / den
        cat_ref[:, Wm + h * LANES:Wm + (h + 1) * LANES] = o.astype(BF16)
    o_ref[...] = x_ref[...] + jnp.dot(cat_ref[...], w_ref[...], preferred_element_type=F32)


def _outproj(mix, proj, mq_block, mem_kv, w_out, x):
    T, D = x.shape
    Wm = mix.shape[1]
    Wmem = D - Wm
    M = mem_kv.shape[0]
    tm = _pick_tile(T, 512, 16)
    return pl.pallas_call(
        _outproj_kernel,
        grid=(T // tm,),
        in_specs=[pl.BlockSpec((tm, Wm), lambda i: (i, 0)),
                  pl.BlockSpec((tm, Wmem), lambda i: (i, mq_block)),
                  pl.BlockSpec((M, Wmem), lambda i: (0, 0)),
                  pl.BlockSpec((M, Wmem), lambda i: (0, 1)),
                  pl.BlockSpec((D, D), lambda i: (0, 0)),
                  pl.BlockSpec((tm, D), lambda i: (i, 0))],
        out_specs=pl.BlockSpec((tm, D), lambda i: (i, 0)),
        out_shape=jax.ShapeDtypeStruct((T, D), F32),
        scratch_shapes=[pltpu.VMEM((tm, D), BF16)],
        compiler_params=_cparams("parallel"),
        name="outproj",
    )(mix, proj, mem_kv, mem_kv, w_out, x)


def _ffn_kernel(x_ref, xh_ref, g_ref, wg_ref, wu_ref, cw_ref, cb_ref, wd_ref, fw_ref, o_ref,
                h_ref, hh_ref, acc_ref, *, final_norm):
    i, f = pl.program_id(0), pl.program_id(1)
    tm, tf = acc_ref.shape[0], wg_ref.shape[1]

    @pl.when(f == 0)
    def _():
        _rms_rows(x_ref, g_ref, h_ref)
        _rms_rows(xh_ref, g_ref, hh_ref)
        hh_ref[...] = jnp.where(i > 0, hh_ref[...], jnp.zeros_like(hh_ref))
        acc_ref[...] = jnp.zeros(acc_ref.shape, F32)

    h = h_ref[...]
    gate = jnp.dot(h, wg_ref[...], preferred_element_type=F32)
    halo = jnp.dot(hh_ref[...], wg_ref[...], preferred_element_type=F32)
    rows = hh_ref.shape[0]
    prev1 = jnp.broadcast_to(halo[rows - 1:rows, :], (tm, tf))
    prev2 = jnp.broadcast_to(halo[rows - 2:rows - 1, :], (tm, tf))
    row = lax.broadcasted_iota(I32, (tm, tf), 0)
    g1 = jnp.where(row < 1, prev1, pltpu.roll(gate, 1, 0))
    g2 = jnp.where(row < 1, prev2, jnp.where(row < 2, prev1, pltpu.roll(gate, 2, 0)))
    cw = cw_ref[...]
    gt = cw[0:1, :] * g2 + cw[1:2, :] * g1 + cw[2:3, :] * gate + cb_ref[...]
    up = jnp.dot(h, wu_ref[...], preferred_element_type=F32)
    act = (gt * _sigmoid(gt) * up).astype(BF16)
    acc_ref[...] += jnp.dot(act, wd_ref[...], preferred_element_type=F32)

    @pl.when(f == pl.num_programs(1) - 1)
    def _():
        if final_norm:
            acc_ref[...] += x_ref[...]
            _rms_rows(acc_ref, fw_ref, o_ref)
        else:
            o_ref[...] = x_ref[...] + acc_ref[...]


def _ffn(x, ln_w, w_gate, w_up, conv_w, conv_b, w_down, final_w, *, final_norm):
    T, D = x.shape
    F = w_gate.shape[1]
    tm = _pick_tile(T, 512, 16)
    tf = _pick_tile(F, 512)
    halo = 8
    vec = lambda n: pl.BlockSpec((1, n), lambda i, f: (0, 0))
    return pl.pallas_call(
        functools.partial(_ffn_kernel, final_norm=final_norm),
        grid=(T // tm, F // tf),
        in_specs=[pl.BlockSpec((tm, D), lambda i, f: (i, 0)),
                  pl.BlockSpec((halo, D), lambda i, f: (jnp.maximum(i * (tm // halo) - 1, 0), 0)),
                  vec(D),
                  pl.BlockSpec((D, tf), lambda i, f: (0, f)),
                  pl.BlockSpec((D, tf), lambda i, f: (0, f)),
                  pl.BlockSpec((CONV_WIDTH, tf), lambda i, f: (0, f)),
                  pl.BlockSpec((1, tf), lambda i, f: (0, f)),
                  pl.BlockSpec((tf, D), lambda i, f: (f, 0)),
                  vec(D)],
        out_specs=pl.BlockSpec((tm, D), lambda i, f: (i, 0)),
        out_shape=jax.ShapeDtypeStruct((T, D), F32),
        scratch_shapes=[pltpu.VMEM((tm, D), BF16), pltpu.VMEM((halo, D), BF16), pltpu.VMEM((tm, D), F32)],
        compiler_params=_cparams("parallel", "arbitrary"),
        name="conv_ffn",
    )(x, x, ln_w.reshape(1, D), w_gate, w_up, conv_w, conv_b.reshape(1, F), w_down, final_w.reshape(1, D))


def _dsa_weight_layout(w, Wm, Wmem):
    n_iq = IDX_HEADS * IDX_DIM
    o = 3 * Wm + n_iq
    ik_iw = w[:, o:o + IDX_DIM + IDX_HEADS]
    mq = w[:, o + IDX_DIM + IDX_HEADS:]
    pad = jnp.zeros((w.shape[0], LANES - IDX_DIM - IDX_HEADS), w.dtype)
    return jnp.concatenate([w[:, :o], mq, ik_iw, pad], axis=1)


def _trunk(x, mem, pos, ln1_w, ln2_w, dsa_w_in, hgrn_w_in, hgrn_lb_logits, hgrn_gnorm_w, mem_norm_w,
           w_mem_kv, w_out, ffn_w_gate, ffn_w_up, ffn_conv_w, ffn_conv_b, ffn_w_down, final_norm_w):
    T, D = x.shape
    Wmem = w_mem_kv.shape[1] // 2
    Wm = D - Wmem
    H = Wm // HEAD_DIM
    depth = ln1_w.shape[0]
    n_sel = min(TOPK_MAX, T // 4)

    mem_kv = _norm_matmul(mem, mem_norm_w, w_mem_kv.astype(BF16), out_dtype=BF16, tn_cap=Wmem, name="mem_kv")
    cos_t, sin_t = _rope_tables(pos.reshape(T, 1))

    for layer in range(depth):
        j = layer // N_MIXERS
        if layer % N_MIXERS == 0:
            w_in = _dsa_weight_layout(dsa_w_in[j], Wm, Wmem).astype(BF16)
            proj = _norm_matmul(x, ln1_w[layer], w_in, out_dtype=F32, tn_cap=1024, name="dsa_proj")
            qh, kh, vh, iq, ika, ikb, iw = _dsa_prep(proj, cos_t, sin_t, H=H)
            bq_idx = min(128, T)
            kc = min(512, T)
            mask = _indexer(iq, ika, ikb, iw, bq=bq_idx, kc=kc, n_sel=n_sel)
            mix = _flash(qh, kh, vh, mask, bq=min(512, T), bk=kc)
            mq_block = (3 * Wm + IDX_HEADS * IDX_DIM) // Wmem
        else:
            proj = _norm_matmul(x, ln1_w[layer], hgrn_w_in[j].astype(BF16), out_dtype=F32, tn_cap=1024,
                                name="hgrn_proj")
            mix = _hgrn(proj, hgrn_lb_logits, hgrn_gnorm_w[j], H=H, layer=layer)
            mq_block = 4 * Wm // Wmem
        x = _outproj(mix, proj, mq_block, mem_kv, w_out[layer].astype(BF16), x)
        x = _ffn(x, ln2_w[layer], ffn_w_gate[layer].astype(BF16), ffn_w_up[layer].astype(BF16),
                 ffn_conv_w[layer], ffn_conv_b[layer], ffn_w_down[layer].astype(BF16), final_norm_w,
                 final_norm=(layer == depth - 1))
    return x


def kernel(x, mem, positions, ln1_w, ln2_w, dsa_w_in, hgrn_w_in, hgrn_lb_logits, hgrn_gnorm_w, mem_norm_w,
           w_mem_kv, w_out, ffn_w_gate, ffn_w_up, ffn_conv_w, ffn_conv_b, ffn_w_down, final_norm_w):
    outs = [_trunk(x[b], mem[b], positions[b], ln1_w, ln2_w, dsa_w_in, hgrn_w_in, hgrn_lb_logits,
                   hgrn_gnorm_w, mem_norm_w, w_mem_kv, w_out, ffn_w_gate, ffn_w_up, ffn_conv_w,
                   ffn_conv_b, ffn_w_down, final_norm_w)
            for b in range(x.shape[0])]
    return jnp.stack(outs, axis=0)
```
